```python
import math
import jax
import jax.numpy as jnp
from jax import lax
import numpy as np

D_MODEL = 2048
BATCH = 4
SEQ = 2048
DEPTH = 2

CTX_LEN = 256
GRID_W = 64
EPS = 1e-6

SSD_HEADS = 16
SSD_HEAD_DIM = 64
SSD_INNER = SSD_HEADS * SSD_HEAD_DIM
SSD_GROUPS = 4
SSD_HPG = SSD_HEADS // SSD_GROUPS
SSD_STATE = 128
SSD_CONV = 5
SSD_CHUNK = 128
SSD_BC = SSD_GROUPS * SSD_STATE
SSD_CONV_CH = SSD_INNER + 2 * SSD_BC
SSD_IN = SSD_INNER + SSD_CONV_CH + SSD_HEADS

MLA_HEADS = 16
MLA_Q_RANK = 512
MLA_KV_RANK = 256
MLA_NOPE = 64
MLA_ROPE = 32
MLA_V = 64
MLA_INNER = MLA_HEADS * MLA_V
MLA_IN = MLA_Q_RANK + MLA_KV_RANK + MLA_ROPE
ATTN_BLOCK = 128
ROPE_BASE = 10000.0

S5_INNER = 1024
S5_GROUP = 16
S5_GROUPS = S5_INNER // S5_GROUP
S5_STATE = 64

N_BRANCH = 3
GATE_IN = N_BRANCH * D_MODEL
PROJ_IN = SSD_IN + MLA_IN + S5_INNER + GATE_IN

N_EXPERTS = 16
EXPERT_FF = 2048
CAPACITY_FACTOR = 2

kernel_name = 'hybrid_ssd_mla_s5_ec_dit'


def rms_norm(x, gain):
    xf = x.astype(jnp.float32)
    y = xf * lax.rsqrt(jnp.mean(xf * xf, axis=-1, keepdims=True) + EPS)
    return (y * gain.astype(jnp.float32)).astype(x.dtype)


def _flip(t):
    return jnp.flip(t, axis=1)


def _ident(t):
    return t


def dw_conv(x, w, b):
    k = w.shape[0]
    pad = (k - 1) // 2
    y = lax.conv_general_dilated(x, w[:, None, :].astype(x.dtype), window_strides=(1,),
                                 padding=[(pad, pad)], dimension_numbers=('NWC', 'WIO', 'NWC'),
                                 feature_group_count=x.shape[-1])
    return y + b.astype(x.dtype)


def axial_rope_tables(n_rows):
    n_freq = MLA_ROPE // 4
    inv = ROPE_BASE ** (-jnp.arange(n_freq, dtype=jnp.float32) / n_freq)
    rows = jnp.repeat(jnp.arange(n_rows, dtype=jnp.float32), GRID_W)
    cols = jnp.tile(jnp.arange(GRID_W, dtype=jnp.float32), n_rows)
    ang = jnp.concatenate([rows[:, None] * inv, cols[:, None] * inv], axis=-1)
    return jnp.cos(ang), jnp.sin(ang)


def apply_rope(x, cos, sin):
    x1 = x[..., 0::2]
    x2 = x[..., 1::2]
    cos = cos.astype(x.dtype)
    sin = sin.astype(x.dtype)
    return jnp.stack([x1 * cos - x2 * sin, x1 * sin + x2 * cos], axis=-1).reshape(x.shape)


def ssd_chunked(x, dt, a_neg, bm, cm, h0):
    b, l, g, hg, p = x.shape
    n = bm.shape[-1]
    nc = l // SSD_CHUNK
    q = SSD_CHUNK
    xf = (x * dt[..., None]).reshape(b, nc, q, g, hg, p)
    a = (dt * a_neg).reshape(b, nc, q, g, hg)
    bc = bm.reshape(b, nc, q, g, n)
    cc = cm.reshape(b, nc, q, g, n)
    acs = jnp.cumsum(a, axis=2)
    diff = acs[:, :, :, None] - acs[:, :, None, :]
    mask = jnp.tril(jnp.ones((q, q), bool))[:, :, None, None]
    decay = jnp.exp(jnp.where(mask, diff, -jnp.inf))
    scores = jnp.einsum('bclgn,bcsgn->bclsg', cc, bc)
    y_diag = jnp.einsum('bclsgh,bcsghp->bclghp', scores[..., None] * decay, xf)
    xw = xf * jnp.exp(acs[:, :, -1:] - acs)[..., None]
    states = jnp.einsum('bclgn,bclghp->bcghpn', bc, xw)
    chunk_decay = jnp.exp(acs[:, :, -1])

    def step(h, inp):
        s, d = inp
        return h * d[..., None, None] + s, h

    h_final, h_prev = lax.scan(step, h0, (jnp.moveaxis(states, 1, 0), jnp.moveaxis(chunk_decay, 1, 0)))
    h_prev = jnp.moveaxis(h_prev, 0, 1)
    y_off = jnp.einsum('bclgn,bcghpn->bclghp', cc, h_prev) * jnp.exp(acs)[..., None]
    return (y_diag + y_off).reshape(b, l, g, hg, p), h_final


def ssd_mixer(px, pc, conv_w, conv_b, a_log, dt_bias, d_skip, norm_gain, need_ctx):
    f32 = jnp.float32

    def prep(p):
        b, l, _ = p.shape
        z = p[..., :SSD_INNER]
        xbc = jax.nn.silu(dw_conv(p[..., SSD_INNER:SSD_INNER + SSD_CONV_CH], conv_w, conv_b)).astype(f32)
        xs = xbc[..., :SSD_INNER].reshape(b, l, SSD_GROUPS, SSD_HPG, SSD_HEAD_DIM)
        bm = xbc[..., SSD_INNER:SSD_INNER + SSD_BC].reshape(b, l, SSD_GROUPS, SSD_STATE)
        cm = xbc[..., SSD_INNER + SSD_BC:].reshape(b, l, SSD_GROUPS, SSD_STATE)
        dt = p[..., SSD_INNER + SSD_CONV_CH:].astype(f32).reshape(b, l, SSD_GROUPS, SSD_HPG)
        return z, xs, dt, bm, cm

    zx, xx, dtx, bx, cx = prep(px)
    zc, xc, dtc, bc, cc = prep(pc)
    b = px.shape[0]
    yx = jnp.zeros_like(xx)
    yc = jnp.zeros_like(xc)
    for d in range(2):
        flip = _flip if d == 1 else _ident
        a_neg = -jnp.exp(a_log[d].astype(f32)).reshape(SSD_GROUPS, SSD_HPG)
        bias = dt_bias[d].astype(f32).reshape(SSD_GROUPS, SSD_HPG)
        dsk = d_skip[d].astype(f32).reshape(SSD_GROUPS, SSD_HPG)[..., None]
        dt_c = jax.nn.softplus(dtc + bias)
        dt_x = jax.nn.softplus(dtx + bias)
        h0 = jnp.zeros((b, SSD_GROUPS, SSD_HPG, SSD_HEAD_DIM, SSD_STATE), f32)
        y_c, h_c = ssd_chunked(flip(xc), flip(dt_c), a_neg, flip(bc), flip(cc), h0)
        y_x, _ = ssd_chunked(flip(xx), flip(dt_x), a_neg, flip(bx), flip(cx), h_c)
        yx = yx + flip(y_x) + dsk * xx
        yc = yc + flip(y_c) + dsk * xc

    def finish(y, z):
        y = y.reshape(z.shape) * jax.nn.silu(z.astype(f32))
        return rms_norm(y, norm_gain).astype(z.dtype)

    return finish(yx, zx), (finish(yc, zc) if need_ctx else None)


def block_attention(q_nope, q_pe, k_nope, k_pe, v):
    b, lq, h, _ = q_nope.shape
    nb = lq // ATTN_BLOCK
    scale = (MLA_NOPE + MLA_ROPE) ** -0.5

    def blocks(t):
        return jnp.moveaxis(t.reshape((b, nb, ATTN_BLOCK) + t.shape[2:]), 1, 0)

    def one(qs):
        qn, qp = qs
        s = jnp.einsum('bqhd,bkhd->bhqk', qn, k_nope) + jnp.einsum('bqhr,bkr->bhqk', qp, k_pe)
        p = jax.nn.softmax(s.astype(jnp.float32) * scale, axis=-1).astype(v.dtype)
        return jnp.einsum('bhqk,bkhd->bqhd', p, v)

    o = lax.map(one, (blocks(q_nope), blocks(q_pe)))
    return jnp.moveaxis(o, 0, 1).reshape(b, lq, h * v.shape[-1])


def mla_mixer(px, pc, q_a_gain, kv_a_gain, w_q_b, w_kv_b, q_gain, k_gain, cos, sin, need_ctx):
    def prep(p):
        b, l, _ = p.shape
        q_a = p[..., :MLA_Q_RANK]
        kv_a = p[..., MLA_Q_RANK:MLA_Q_RANK + MLA_KV_RANK]
        k_pe = p[..., MLA_Q_RANK + MLA_KV_RANK:]
        q = (rms_norm(q_a, q_a_gain) @ w_q_b).reshape(b, l, MLA_HEADS, MLA_NOPE + MLA_ROPE)
        kv = (rms_norm(kv_a, kv_a_gain) @ w_kv_b).reshape(b, l, MLA_HEADS, MLA_NOPE + MLA_V)
        q_nope = rms_norm(q[..., :MLA_NOPE], q_gain[:MLA_NOPE])
        q_pe = rms_norm(q[..., MLA_NOPE:], q_gain[MLA_NOPE:])
        k_nope = rms_norm(kv[..., :MLA_NOPE], k_gain[:MLA_NOPE])
        k_pe = rms_norm(k_pe, k_gain[MLA_NOPE:])
        return q_nope, q_pe, k_nope, k_pe, kv[..., MLA_NOPE:]

    qn_x, qp_x, kn_x, kp_x, v_x = prep(px)
    qp_x = apply_rope(qp_x, cos[:, None], sin[:, None])
    kp_x = apply_rope(kp_x, cos, sin)
    qn_c, qp_c, kn_c, kp_c, v_c = prep(pc)
    out_x = block_attention(qn_x, qp_x, jnp.concatenate([kn_x, kn_c], axis=1),
                            jnp.concatenate([kp_x, kp_c], axis=1), jnp.concatenate([v_x, v_c], axis=1))
    out_c = block_attention(qn_c, qp_c, kn_c, kp_c, v_c) if need_ctx else None
    return out_x, out_c


def s5_discretise(lam_re, lam_im, log_dt, b_re, b_im):
    f32 = jnp.float32
    lam_re = lam_re.astype(f32)
    lam_im = lam_im.astype(f32)
    dt = jnp.exp(log_dt.astype(f32))[:, None]
    mag = jnp.exp(lam_re * dt)
    ab_re = mag * jnp.cos(lam_im * dt)
    ab_im = mag * jnp.sin(lam_im * dt)
    den = lam_re * lam_re + lam_im * lam_im
    nr = ab_re - 1.0
    f_re = ((nr * lam_re + ab_im * lam_im) / den)[..., None]
    f_im = ((ab_im * lam_re - nr * lam_im) / den)[..., None]
    b_re = b_re.astype(f32)
    b_im = b_im.astype(f32)
    return ab_re, ab_im, f_re * b_re - f_im * b_im, f_re * b_im + f_im * b_re


def _complex_affine_combine(e1, e2):
    a1r, a1i, b1r, b1i = e1
    a2r, a2i, b2r, b2i = e2
    return (a2r * a1r - a2i * a1i, a2r * a1i + a2i * a1r,
            a2r * b1r - a2i * b1i + b2r, a2r * b1i + a2i * b1r + b2i)


def s5_scan(u, ab_re, ab_im, bb_re, bb_im, h0_re, h0_im):
    bu_re = jnp.einsum('blgs,gps->blgp', u, bb_re)
    bu_im = jnp.einsum('blgs,gps->blgp', u, bb_im)
    bu_re = bu_re.at[:, 0].add(ab_re * h0_re - ab_im * h0_im)
    bu_im = bu_im.at[:, 0].add(ab_re * h0_im + ab_im * h0_re)
    l = u.shape[1]
    a_re = jnp.broadcast_to(ab_re, (1, l) + ab_re.shape)
    a_im = jnp.broadcast_to(ab_im, (1, l) + ab_im.shape)
    _, _, xr, xi = lax.associative_scan(_complex_affine_combine, (a_re, a_im, bu_re, bu_im), axis=1)
    return xr, xi


def s5_readout(xr, xi, c_re, c_im):
    return (jnp.einsum('blgp,gsp->blgs', xr, c_re.astype(jnp.float32))
            - jnp.einsum('blgp,gsp->blgs', xi, c_im.astype(jnp.float32)))


def s5_mixer(ux, uc, lam_re, lam_im, log_dt, b_re, b_im, c_re, c_im, d_skip, w_glu, need_ctx):
    f32 = jnp.float32
    b, l, _ = ux.shape
    lc = uc.shape[1]
    ugx = ux.astype(f32).reshape(b, l, S5_GROUPS, S5_GROUP)
    ugc = uc.astype(f32).reshape(b, lc, S5_GROUPS, S5_GROUP)
    yx = jnp.zeros_like(ugx)
    yc = jnp.zeros_like(ugc)
    for d in range(2):
        flip = _flip if d == 1 else _ident
        ab_re, ab_im, bb_re, bb_im = s5_discretise(lam_re[d], lam_im[d], log_dt[d], b_re[d], b_im[d])
        h0 = jnp.zeros((b, S5_GROUPS, S5_STATE), f32)
        sc_re, sc_im = s5_scan(flip(ugc), ab_re, ab_im, bb_re, bb_im, h0, h0)
        sx_re, sx_im = s5_scan(flip(ugx), ab_re, ab_im, bb_re, bb_im, sc_re[:, -1], sc_im[:, -1])
        yx = yx + flip(s5_readout(sx_re, sx_im, c_re[d], c_im[d]))
        if need_ctx:
            yc = yc + flip(s5_readout(sc_re, sc_im, c_re[d], c_im[d]))

    def finish(y, u):
        y = y.reshape(u.shape) + d_skip.astype(f32) * u.astype(f32)
        y = jax.nn.gelu(y)
        y = y * jax.nn.sigmoid(y @ w_glu.astype(f32))
        return y.astype(u.dtype)

    return finish(yx, ux), (finish(yc, uc) if need_ctx else None)


def merge_branches(y_ssd, y_mla, y_s5, gate_logits, w_b_ssd, w_b_mla, w_b_s5, w_o):
    g = jax.nn.sigmoid(gate_logits.astype(jnp.float32)).astype(y_ssd.dtype)
    g_ssd, g_mla, g_s5 = jnp.split(g, N_BRANCH, axis=-1)
    m = g_ssd * (y_ssd @ w_b_ssd) + g_mla * (y_mla @ w_b_mla) + g_s5 * (y_s5 @ w_b_s5)
    return m @ w_o


def ec_moe(h, w_router, w_gate, w_up, w_down):
    b, n, _ = h.shape
    cap = CAPACITY_FACTOR * n // N_EXPERTS
    aff = jax.nn.softmax(jnp.einsum('bnd,de->bne', h, w_router).astype(jnp.float32), axis=-1)
    g, idx = lax.top_k(jnp.swapaxes(aff, 1, 2), cap)
    bidx = jnp.arange(b)[:, None, None]
    xs = h[bidx, idx]
    hid = jax.nn.silu(jnp.einsum('becd,edf->becf', xs, w_gate)) * jnp.einsum('becd,edf->becf', xs, w_up)
    ys = jnp.einsum('becf,efd->becd', hid, w_down) * g[..., None].astype(h.dtype)
    return jnp.zeros_like(h).at[bidx, idx].add(ys)


def setup_inputs(seed: int = 0) -> dict:
    key = jax.random.key(seed)
    keys = jax.random.split(key, 64)
    counter = [0]

    def nk():
        k = keys[counter[0]]
        counter[0] += 1
        return k

    def nrm(shape, scale):
        return scale * jax.random.normal(nk(), shape, jnp.float32)

    def gain(shape):
        return 1.0 + nrm(shape, 0.02)

    def unif(shape, lo, hi):
        return jax.random.uniform(nk(), shape, jnp.float32, lo, hi)

    dt0 = jnp.exp(unif((DEPTH, 2, SSD_HEADS), math.log(1e-3), math.log(1e-1)))
    n_idx = jnp.arange(S5_STATE, dtype=jnp.float32)
    inp = {}
    inp['x'] = nrm((BATCH, SEQ, D_MODEL), 1.0)
    inp['c'] = nrm((BATCH, D_MODEL), 1.0)
    inp['ctx'] = nrm((BATCH, CTX_LEN, D_MODEL), 1.0)
    inp['c_ctx'] = nrm((D_MODEL,), 1.0)
    inp['norm1_gain'] = gain((DEPTH, D_MODEL))
    inp['norm2_gain'] = gain((DEPTH, D_MODEL))
    inp['w_mod'] = nrm((DEPTH, D_MODEL, 6 * D_MODEL), 0.2 * D_MODEL ** -0.5)
    inp['b_mod'] = nrm((DEPTH, 6 * D_MODEL), 0.02)
    inp['w_in'] = nrm((DEPTH, D_MODEL, PROJ_IN), D_MODEL ** -0.5)
    inp['ssd_conv_w'] = nrm((DEPTH, SSD_CONV, SSD_CONV_CH), SSD_CONV ** -0.5)
    inp['ssd_conv_b'] = nrm((DEPTH, SSD_CONV_CH), 0.02)
    inp['ssd_a_log'] = jnp.log(unif((DEPTH, 2, SSD_HEADS), 1.0, 16.0))
    inp['ssd_dt_bias'] = dt0 + jnp.log(-jnp.expm1(-dt0))
    inp['ssd_d'] = gain((DEPTH, 2, SSD_HEADS))
    inp['ssd_norm_gain'] = gain((DEPTH, SSD_INNER))
    inp['mla_q_a_gain'] = gain((DEPTH, MLA_Q_RANK))
    inp['mla_kv_a_gain'] = gain((DEPTH, MLA_KV_RANK))
    inp['mla_w_q_b'] = nrm((DEPTH, MLA_Q_RANK, MLA_HEADS * (MLA_NOPE + MLA_ROPE)), MLA_Q_RANK ** -0.5)
    inp['mla_w_kv_b'] = nrm((DEPTH, MLA_KV_RANK, MLA_HEADS * (MLA_NOPE + MLA_V)), MLA_KV_RANK ** -0.5)
    inp['mla_q_gain'] = gain((DEPTH, MLA_NOPE + MLA_ROPE))
    inp['mla_k_gain'] = gain((DEPTH, MLA_NOPE + MLA_ROPE))
    inp['s5_lam_re'] = -0.5 + nrm((DEPTH, 2, S5_GROUPS, S5_STATE), 0.01)
    inp['s5_lam_im'] = math.pi * n_idx + nrm((DEPTH, 2, S5_GROUPS, S5_STATE), 0.01)
    inp['s5_log_dt'] = unif((DEPTH, 2, S5_GROUPS), math.log(1e-3), math.log(1e-1))
    inp['s5_b_re'] = nrm((DEPTH, 2, S5_GROUPS, S5_STATE, S5_GROUP), (2 * S5_GROUP) ** -0.5)
    inp['s5_b_im'] = nrm((DEPTH, 2, S5_GROUPS, S5_STATE, S5_GROUP), (2 * S5_GROUP) ** -0.5)
    inp['s5_c_re'] = nrm((DEPTH, 2, S5_GROUPS, S5_GROUP, S5_STATE), (2 * S5_STATE) ** -0.5)
    inp['s5_c_im'] = nrm((DEPTH, 2, S5_GROUPS, S5_GROUP, S5_STATE), (2 * S5_STATE) ** -0.5)
    inp['s5_d'] = nrm((DEPTH, S5_INNER), 1.0)
    inp['s5_w_glu'] = nrm((DEPTH, S5_INNER, S5_INNER), S5_INNER ** -0.5)
    inp['w_branch_ssd'] = nrm((DEPTH, SSD_INNER, D_MODEL), SSD_INNER ** -0.5)
    inp['w_branch_mla'] = nrm((DEPTH, MLA_INNER, D_MODEL), MLA_INNER ** -0.5)
    inp['w_branch_s5'] = nrm((DEPTH, S5_INNER, D_MODEL), S5_INNER ** -0.5)
    inp['w_out'] = nrm((DEPTH, D_MODEL, D_MODEL), D_MODEL ** -0.5)
    inp['moe_router'] = nrm((DEPTH, D_MODEL, N_EXPERTS), D_MODEL ** -0.5)
    inp['moe_w_gate'] = nrm((DEPTH, N_EXPERTS, D_MODEL, EXPERT_FF), D_MODEL ** -0.5)
    inp['moe_w_up'] = nrm((DEPTH, N_EXPERTS, D_MODEL, EXPERT_FF), D_MODEL ** -0.5)
    inp['moe_w_down'] = nrm((DEPTH, N_EXPERTS, EXPERT_FF, D_MODEL), EXPERT_FF ** -0.5)
    return inp


def reference(x, c, ctx, c_ctx, norm1_gain, norm2_gain, w_mod, b_mod, w_in,
              ssd_conv_w, ssd_conv_b, ssd_a_log, ssd_dt_bias, ssd_d, ssd_norm_gain,
              mla_q_a_gain, mla_kv_a_gain, mla_w_q_b, mla_w_kv_b, mla_q_gain, mla_k_gain,
              s5_lam_re, s5_lam_im, s5_log_dt, s5_b_re, s5_b_im, s5_c_re, s5_c_im, s5_d, s5_w_glu,
              w_branch_ssd, w_branch_mla, w_branch_s5, w_out,
              moe_router, moe_w_gate, moe_w_up, moe_w_down):
    n_tok = x.shape[1]
    n_rows = n_tok // GRID_W
    cos, sin = axial_rope_tables(n_rows)
    c_act = jax.nn.silu(c)
    cc_act = jax.nn.silu(c_ctx)
    xc = ctx
    o1 = SSD_IN
    o2 = o1 + MLA_IN
    o3 = o2 + S5_INNER
    for i in range(DEPTH):
        need_ctx = i < DEPTH - 1
        mod_x = (c_act @ w_mod[i] + b_mod[i])[:, None, :]
        mod_c = (cc_act @ w_mod[i] + b_mod[i])[None, None, :]
        shx1, scx1, gx1, shx2, scx2, gx2 = jnp.split(mod_x, 6, axis=-1)
        shc1, scc1, gc1, shc2, scc2, gc2 = jnp.split(mod_c, 6, axis=-1)
        hx = rms_norm(x, norm1_gain[i]) * (1.0 + scx1) + shx1
        hc = rms_norm(xc, norm1_gain[i]) * (1.0 + scc1) + shc1
        px = hx @ w_in[i]
        pc = hc @ w_in[i]
        ssd_x, ssd_c = ssd_mixer(px[..., :o1], pc[..., :o1], ssd_conv_w[i], ssd_conv_b[i], ssd_a_log[i],
                                 ssd_dt_bias[i], ssd_d[i], ssd_norm_gain[i], need_ctx)
        mla_x, mla_c = mla_mixer(px[..., o1:o2], pc[..., o1:o2], mla_q_a_gain[i], mla_kv_a_gain[i],
                                 mla_w_q_b[i], mla_w_kv_b[i], mla_q_gain[i], mla_k_gain[i], cos, sin, need_ctx)
        s5_x, s5_c = s5_mixer(px[..., o2:o3], pc[..., o2:o3], s5_lam_re[i], s5_lam_im[i], s5_log_dt[i],
                              s5_b_re[i], s5_b_im[i], s5_c_re[i], s5_c_im[i], s5_d[i], s5_w_glu[i], need_ctx)
        x = x + gx1 * merge_branches(ssd_x, mla_x, s5_x, px[..., o3:], w_branch_ssd[i],
                                     w_branch_mla[i], w_branch_s5[i], w_out[i])
        if need_ctx:
            xc = xc + gc1 * merge_branches(ssd_c, mla_c, s5_c, pc[..., o3:], w_branch_ssd[i],
                                           w_branch_mla[i], w_branch_s5[i], w_out[i])
        hx2 = rms_norm(x, norm2_gain[i]) * (1.0 + scx2) + shx2
        x = x + gx2 * ec_moe(hx2, moe_router[i], moe_w_gate[i], moe_w_up[i], moe_w_down[i])
        if need_ctx:
            hc2 = rms_norm(xc, norm2_gain[i]) * (1.0 + scc2) + shc2
            xc = xc + gc2 * ec_moe(hc2, moe_router[i], moe_w_gate[i], moe_w_up[i], moe_w_down[i])
    return x
```

```python
import functools
import math

import jax
import jax.numpy as jnp
from jax import lax
from jax.experimental import pallas as pl
from jax.experimental.pallas import tpu as pltpu

F32 = jnp.float32
BF16 = jnp.bfloat16
HIGHEST = lax.Precision.HIGHEST

D = 2048
NB = 4
LX = 2048
LC = 256
NX = NB * LX
NC = NB * LC
NT = NX + NC
DEPTH = 2
EPS = 1e-6
GRID_W = 64

SSD_HEADS = 16
SSD_P = 64
SSD_INNER = 1024
SSD_G = 4
SSD_HPG = 4
SSD_N = 128
SSD_Q = 128
SSD_BC = SSD_G * SSD_N
SSD_CONV_CH = SSD_INNER + 2 * SSD_BC
SSD_IN = SSD_INNER + SSD_CONV_CH + SSD_HEADS

MLA_H = 16
MLA_QR = 512
MLA_KVR = 256
MLA_NOPE = 64
MLA_ROPE = 32
MLA_V = 64
MLA_IN = MLA_QR + MLA_KVR + MLA_ROPE
ROPE_BASE = 10000.0

S5_INNER = 1024
S5_S = 16
S5_GROUPS = 64
S5_P = 64
S5_T = 8
S5_GB = 8

N_EXP = 16
FF = 2048
CAP_X = 2 * LX // N_EXP
CAP_C = 2 * LC // N_EXP

O1 = SSD_IN
O2 = O1 + MLA_IN
O3 = O2 + S5_INNER

VMEM_LIMIT = 56 * 1024 * 1024


def _params(sem):
    return pltpu.CompilerParams(dimension_semantics=sem, vmem_limit_bytes=VMEM_LIMIT)


def _mod_row(i, tiles_per_batch):
    return jnp.minimum(i // tiles_per_batch, NB)


def _modulation_kernel(c_ref, w_ref, b_ref, o_ref):
    c = c_ref[...]
    a = (c * jax.nn.sigmoid(c)).astype(BF16)
    o_ref[0] = jnp.dot(a, w_ref[0].astype(BF16), preferred_element_type=F32) + b_ref[0]


def modulation(cvec, w_mod, b_mod):
    tn = 1536
    n = w_mod.shape[-1]
    return pl.pallas_call(
        _modulation_kernel,
        grid=(DEPTH, n // tn),
        in_specs=[pl.BlockSpec((8, D), lambda l, j: (0, 0)),
                  pl.BlockSpec((1, D, tn), lambda l, j: (l, 0, j)),
                  pl.BlockSpec((1, 1, tn), lambda l, j: (l, 0, j))],
        out_specs=pl.BlockSpec((1, 8, tn), lambda l, j: (l, 0, j)),
        out_shape=jax.ShapeDtypeStruct((DEPTH, 8, n), F32),
        compiler_params=_params(("parallel", "parallel")),
        name="modulation",
    )(cvec, w_mod, b_mod.reshape(DEPTH, 1, n))


def _normmod_kernel(x_ref, g_ref, sc_ref, sh_ref, o_ref):
    x = x_ref[...]
    ms = jnp.mean(x * x, axis=-1, keepdims=True)
    y = x * lax.rsqrt(ms + EPS) * g_ref[...]
    o_ref[...] = (y * (1.0 + sc_ref[0]) + sh_ref[0]).astype(o_ref.dtype)


def normmod(x, gain, modtab, k_shift, k_scale, rows):
    tm = 256
    tpb = LX // tm
    return pl.pallas_call(
        _normmod_kernel,
        grid=(rows // tm,),
        in_specs=[pl.BlockSpec((tm, D), lambda i: (i, 0)),
                  pl.BlockSpec((1, D), lambda i: (0, 0)),
                  pl.BlockSpec((1, 1, D), lambda i: (_mod_row(i, tpb) * 6 + k_scale, 0, 0)),
                  pl.BlockSpec((1, 1, D), lambda i: (_mod_row(i, tpb) * 6 + k_shift, 0, 0))],
        out_specs=pl.BlockSpec((tm, D), lambda i: (i, 0)),
        out_shape=jax.ShapeDtypeStruct((rows, D), BF16),
        compiler_params=_params(("parallel",)),
        name="normmod",
    )(x, gain.reshape(1, D), modtab, modtab)


def _mm_kernel(a_ref, w_ref, o_ref, *, act):
    acc = jnp.dot(a_ref[...], w_ref[...], preferred_element_type=F32)
    if act == "sigmoid":
        acc = jax.nn.sigmoid(acc)
    o_ref[...] = acc.astype(o_ref.dtype)


def matmul(a, w, tm, tn, act=None, out_dtype=F32, name="matmul", rows=None):
    m, k = a.shape
    m = rows or m
    n = w.shape[1]
    return pl.pallas_call(
        functools.partial(_mm_kernel, act=act),
        grid=(m // tm, n // tn),
        in_specs=[pl.BlockSpec((tm, k), lambda i, j: (i, 0)),
                  pl.BlockSpec((k, tn), lambda i, j: (0, j))],
        out_specs=pl.BlockSpec((tm, tn), lambda i, j: (i, j)),
        out_shape=jax.ShapeDtypeStruct((m, n), out_dtype),
        compiler_params=_params(("parallel", "parallel")),
        name=name,
    )(a, w)


def _mm_nt_kernel(w_ref, a_ref, o_ref):
    o_ref[...] = lax.dot_general(w_ref[...], a_ref[...], (((1,), (1,)), ((), ())),
                                 preferred_element_type=F32)


def matmul_nt(w_rows, a, tm):
    r, k = w_rows.shape
    m = a.shape[0]
    return pl.pallas_call(
        _mm_nt_kernel,
        grid=(m // tm,),
        in_specs=[pl.BlockSpec((r, k), lambda i: (0, 0)),
                  pl.BlockSpec((tm, k), lambda i: (i, 0))],
        out_specs=pl.BlockSpec((r, tm), lambda i: (0, i)),
        out_shape=jax.ShapeDtypeStruct((r, m), F32),
        compiler_params=_params(("parallel",)),
        name="matmul_nt",
    )(w_rows, a)


CONV_TM = 256
CONV_TC = 512
CONV_HALO = 8


def _conv_kernel(prev_ref, cur_ref, next_ref, w_ref, b_ref, o_ref, ext_ref):
    i = pl.program_id(0)
    blocks_per_seq = LX // CONV_TM
    is_ctx = i >= NX // CONV_TM
    first = jnp.logical_or(is_ctx, i % blocks_per_seq == 0)
    last = jnp.logical_or(is_ctx, i % blocks_per_seq == blocks_per_seq - 1)
    zeros = jnp.zeros((CONV_HALO, CONV_TC), F32)
    ext_ref[0:CONV_HALO, :] = jnp.where(first, zeros, prev_ref[...])
    ext_ref[CONV_HALO:CONV_HALO + CONV_TM, :] = cur_ref[...]
    ext_ref[CONV_HALO + CONV_TM:, :] = jnp.where(last, zeros, next_ref[...])
    acc = b_ref[...] + jnp.zeros((CONV_TM, CONV_TC), F32)
    for k in range(5):
        acc = acc + ext_ref[pl.ds(CONV_HALO + k - 2, CONV_TM), :] * w_ref[k:k + 1, :]
    o_ref[...] = acc * jax.nn.sigmoid(acc)


def ssd_conv(p_ssd, conv_w, conv_b):
    nblk = NT // CONV_TM
    hb = CONV_TM // CONV_HALO
    c0 = SSD_INNER // CONV_TC
    w8 = jnp.zeros((8, SSD_CONV_CH), F32).at[:5].set(conv_w)
    return pl.pallas_call(
        _conv_kernel,
        grid=(nblk, SSD_CONV_CH // CONV_TC),
        in_specs=[pl.BlockSpec((CONV_HALO, CONV_TC), lambda i, j: (jnp.maximum(i * hb - 1, 0), c0 + j)),
                  pl.BlockSpec((CONV_TM, CONV_TC), lambda i, j: (i, c0 + j)),
                  pl.BlockSpec((CONV_HALO, CONV_TC),
                               lambda i, j: (jnp.minimum((i + 1) * hb, NT // CONV_HALO - 1), c0 + j)),
                  pl.BlockSpec((8, CONV_TC), lambda i, j: (0, j)),
                  pl.BlockSpec((1, CONV_TC), lambda i, j: (0, j))],
        out_specs=pl.BlockSpec((CONV_TM, CONV_TC), lambda i, j: (i, j)),
        out_shape=jax.ShapeDtypeStruct((NT, SSD_CONV_CH), F32),
        scratch_shapes=[pltpu.VMEM((CONV_TM + 2 * CONV_HALO, CONV_TC), F32)],
        compiler_params=_params(("parallel", "parallel")),
        name="ssd_conv",
    )(p_ssd, p_ssd, p_ssd, w8, conv_b.reshape(1, SSD_CONV_CH))


NCH_C = LC // SSD_Q
NCH_X = LX // SSD_Q
SSD_STEPS = NCH_C + NCH_X


def _softplus(x):
    return jnp.maximum(x, 0.0) + jnp.log1p(jnp.exp(-jnp.abs(x)))


def _ssd_one_direction(d, xs_ref, b_ref, c_ref, dtc_ref, dtr_ref, pcol, prow, st_ref, y_ref, row, col):
    q = SSD_Q
    xs = xs_ref[...]
    bm = b_ref[...].astype(BF16)
    cm = c_ref[...].astype(BF16)
    tri_l = (col <= row).astype(F32)
    tri_u = (col >= row).astype(F32)
    t_col = tri_l if d == 0 else tri_u
    t_row = tri_u if d == 0 else tri_l
    mask = (col <= row) if d == 0 else (col >= row)
    dt_c = _softplus(dtc_ref[...] + pcol[2 + d:3 + d, :])
    a_c = dt_c * (-jnp.exp(pcol[d:d + 1, :]))
    cs_c = jnp.dot(t_col, a_c, precision=HIGHEST, preferred_element_type=F32)
    dt_r = _softplus(dtr_ref[...] + prow[2 + d])
    a_r = dt_r * (-jnp.exp(prow[d]))
    cs_r = jnp.dot(a_r, t_row, precision=HIGHEST, preferred_element_type=F32)
    edge = q - 1 if d == 0 else 0
    tot_c = cs_c[edge:edge + 1, :]
    tot_r = cs_r[:, edge:edge + 1]

    def lanes(v):
        return jnp.concatenate([jnp.broadcast_to(v[:, h:h + 1], (v.shape[0], SSD_P))
                                for h in range(SSD_HPG)], axis=1)

    dt_b = lanes(dt_c)
    cs_b = lanes(cs_c)
    tot_b = lanes(tot_c)
    xdt = xs * dt_b
    scores = lax.dot_general(cm, bm, (((1,), (1,)), ((), ())), preferred_element_type=F32)
    st = st_ref[d]
    y_off = lax.dot_general(cm, st.astype(BF16), (((1,), (1,)), ((), ())),
                            preferred_element_type=F32) * jnp.exp(cs_b)
    xdt16 = xdt.astype(BF16)
    ys = []
    for h in range(SSD_HPG):
        diff = cs_c[:, h:h + 1] - cs_r[h:h + 1, :]
        decay = jnp.exp(jnp.where(mask, diff, -jnp.inf))
        ys.append(jnp.dot((scores * decay).astype(BF16), xdt16[:, h * SSD_P:(h + 1) * SSD_P],
                          preferred_element_type=F32))
    y_ref[...] = jnp.concatenate(ys, axis=1) + y_off
    xw_t = (xdt * jnp.exp(tot_b - cs_b)).T.astype(BF16)
    s_new = jnp.dot(xw_t, bm, preferred_element_type=F32)
    dec = jnp.concatenate([jnp.broadcast_to(jnp.exp(tot_r[h:h + 1, :]), (SSD_P, SSD_N))
                           for h in range(SSD_HPG)], axis=0)
    st_ref[d] = st * dec + s_new


def _ssd_kernel(xf, bf, cf, dtcf, dtrf, xr, br, cr, dtcr, dtrr, pcol_ref, prow_ref, yf_ref, yr_ref, st_ref):
    k = pl.program_id(2)

    @pl.when(k == 0)
    def _():
        st_ref[...] = jnp.zeros_like(st_ref)

    row = lax.broadcasted_iota(jnp.int32, (SSD_Q, SSD_Q), 0)
    col = lax.broadcasted_iota(jnp.int32, (SSD_Q, SSD_Q), 1)
    pcol = pcol_ref[0]
    prow = prow_ref[0]
    _ssd_one_direction(0, xf, bf, cf, dtcf, dtrf, pcol, prow, st_ref, yf_ref, row, col)
    _ssd_one_direction(1, xr, br, cr, dtcr, dtrr, pcol, prow, st_ref, yr_ref, row, col)


def _chunk_fwd(b, k):
    return jnp.where(k < NCH_C, NX // SSD_Q + b * NCH_C + k, b * NCH_X + (k - NCH_C))


def _chunk_rev(b, k):
    return jnp.where(k < NCH_C, NX // SSD_Q + b * NCH_C + (NCH_C - 1 - k),
                     b * NCH_X + (NCH_X - 1 - (k - NCH_C)))


def ssd_scan(xbc, p_ssd, dt_rows, pcol, prow):
    q = SSD_Q
    dtc0 = (SSD_INNER + SSD_CONV_CH) // 128

    def specs(chunk):
        return [pl.BlockSpec((q, 256), lambda b, g, k: (chunk(b, k), g)),
                pl.BlockSpec((q, 128), lambda b, g, k: (chunk(b, k), SSD_INNER // 128 + g)),
                pl.BlockSpec((q, 128), lambda b, g, k: (chunk(b, k), (SSD_INNER + SSD_BC) // 128 + g)),
                pl.BlockSpec((q, 128), lambda b, g, k: (chunk(b, k), dtc0 + g)),
                pl.BlockSpec((8, q), lambda b, g, k: (g, chunk(b, k)))]

    def operands():
        return [xbc, xbc, xbc, p_ssd, dt_rows]

    out_spec_f = pl.BlockSpec((q, 256), lambda b, g, k: (_chunk_fwd(b, k), g))
    out_spec_r = pl.BlockSpec((q, 256), lambda b, g, k: (_chunk_rev(b, k), g))
    return pl.pallas_call(
        _ssd_kernel,
        grid=(NB, SSD_G, SSD_STEPS),
        in_specs=specs(_chunk_fwd) + specs(_chunk_rev) + [
            pl.BlockSpec((1, 8, 128), lambda b, g, k: (g, 0, 0)),
            pl.BlockSpec((1, 4, 8, 128), lambda b, g, k: (g, 0, 0, 0))],
        out_specs=[out_spec_f, out_spec_r],
        out_shape=[jax.ShapeDtypeStruct((NT, SSD_INNER), F32)] * 2,
        scratch_shapes=[pltpu.VMEM((2, SSD_HPG * SSD_P, SSD_N), F32)],
        compiler_params=_params(("parallel", "parallel", "arbitrary")),
        name="ssd_scan",
    )(*operands(), *operands(), pcol, prow)


def _ssd_finish_kernel(yf_ref, yr_ref, xs_ref, z_ref, dsum_ref, g_ref, o_ref):
    z = z_ref[...]
    y = (yf_ref[...] + yr_ref[...] + dsum_ref[...] * xs_ref[...]) * (z * jax.nn.sigmoid(z))
    ms = jnp.mean(y * y, axis=-1, keepdims=True)
    o_ref[...] = (y * lax.rsqrt(ms + EPS) * g_ref[...]).astype(o_ref.dtype)


def ssd_finish(yf, yr, xbc, p_ssd, dsum, gain, rows):
    tm = 512
    blk = lambda: pl.BlockSpec((tm, SSD_INNER), lambda i: (i, 0))
    vec = lambda: pl.BlockSpec((1, SSD_INNER), lambda i: (0, 0))
    return pl.pallas_call(
        _ssd_finish_kernel,
        grid=(rows // tm,),
        in_specs=[blk(), blk(), blk(), blk(), vec(), vec()],
        out_specs=blk(),
        out_shape=jax.ShapeDtypeStruct((rows, SSD_INNER), BF16),
        compiler_params=_params(("parallel",)),
        name="ssd_finish",
    )(yf, yr, xbc, p_ssd, dsum, gain.reshape(1, SSD_INNER))


MLA_TM = 256
LKV = LX + LC


def _mla_prep_kernel(p_ref, wq_ref, wkv_ref, gq_ref, gkv_ref, tq1_ref, tq2_ref, tk1_ref, tk2_ref,
                     ct_ref, st_ref, q_ref, k_ref, v_ref, qn_scr, kvn_scr, kp_scr):
    h = pl.program_id(1)
    lane = lax.broadcasted_iota(jnp.int32, (MLA_TM, 128), 1)
    is_nope = lane < MLA_NOPE
    is_rope = jnp.logical_and(lane >= MLA_NOPE, lane < MLA_NOPE + MLA_ROPE)
    ct = ct_ref[...]
    st = st_ref[...]

    @pl.when(h == 0)
    def _():
        qa = p_ref[:, 0:MLA_QR]
        qn_scr[...] = (qa * lax.rsqrt(jnp.mean(qa * qa, axis=-1, keepdims=True) + EPS)
                       * gq_ref[...]).astype(BF16)
        kva = p_ref[:, MLA_QR:MLA_QR + MLA_KVR]
        kvn_scr[...] = (kva * lax.rsqrt(jnp.mean(kva * kva, axis=-1, keepdims=True) + EPS)
                        * gkv_ref[...]).astype(BF16)
        kp = p_ref[:, MLA_QR + MLA_KVR:]
        ss = jnp.sum(jnp.where(is_rope, kp * kp, 0.0), axis=-1, keepdims=True) * (1.0 / MLA_ROPE)
        r = lax.rsqrt(ss + EPS)
        kp_scr[...] = kp * (r * ct * tk1_ref[1:2, :]) + pltpu.roll(kp, 96, 1) * (r * st * tk2_ref[...])

    rq = jnp.dot(qn_scr[...], wq_ref[0], preferred_element_type=F32)
    sq = rq * rq
    rn = lax.rsqrt(jnp.sum(jnp.where(is_nope, sq, 0.0), axis=-1, keepdims=True) * (1.0 / MLA_NOPE) + EPS)
    rp = lax.rsqrt(jnp.sum(jnp.where(is_rope, sq, 0.0), axis=-1, keepdims=True) * (1.0 / MLA_ROPE) + EPS)
    scale1 = jnp.where(is_nope, rn, rp) * tq1_ref[...] * ct
    scale2 = rp * tq2_ref[...] * st
    q_ref[0, 0] = (rq * scale1 + pltpu.roll(rq, 96, 1) * scale2).astype(BF16)

    rk = jnp.dot(kvn_scr[...], wkv_ref[0], preferred_element_type=F32)
    rkn = lax.rsqrt(jnp.sum(jnp.where(is_nope, rk * rk, 0.0), axis=-1, keepdims=True) * (1.0 / MLA_NOPE) + EPS)
    k_ref[0, 0] = jnp.where(is_nope, rk * rkn * tk1_ref[0:1, :], kp_scr[...]).astype(BF16)
    v_ref[0, 0] = rk.astype(BF16)


def _mla_tile_b(i):
    return jnp.where(i < NX // MLA_TM, i // (LX // MLA_TM), i - NX // MLA_TM)


def _mla_tile_r(i):
    return jnp.where(i < NX // MLA_TM, i % (LX // MLA_TM), LX // MLA_TM)


def mla_prep(p_mla, wq, wkv, gq, gkv, tq1, tq2, tk1, tk2, ctab, stab):
    ntile = NT // MLA_TM
    n_in = p_mla.shape[1]
    vec = lambda r: pl.BlockSpec((r, 128), lambda i, h: (0, 0))
    head_out = lambda: pl.BlockSpec((1, 1, MLA_TM, 128), lambda i, h: (_mla_tile_b(i), h, _mla_tile_r(i), 0))
    shp = jax.ShapeDtypeStruct((NB, MLA_H, LKV, 128), BF16)
    return pl.pallas_call(
        _mla_prep_kernel,
        grid=(ntile, MLA_H),
        in_specs=[pl.BlockSpec((MLA_TM, n_in), lambda i, h: (i, 0)),
                  pl.BlockSpec((1, MLA_QR, 128), lambda i, h: (h, 0, 0)),
                  pl.BlockSpec((1, MLA_KVR, 128), lambda i, h: (h, 0, 0)),
                  pl.BlockSpec((1, MLA_QR), lambda i, h: (0, 0)),
                  pl.BlockSpec((1, MLA_KVR), lambda i, h: (0, 0)),
                  vec(1), vec(1), vec(2), vec(1),
                  pl.BlockSpec((MLA_TM, 128), lambda i, h: (i, 0)),
                  pl.BlockSpec((MLA_TM, 128), lambda i, h: (i, 0))],
        out_specs=[head_out(), head_out(), head_out()],
        out_shape=[shp, shp, shp],
        scratch_shapes=[pltpu.VMEM((MLA_TM, MLA_QR), BF16), pltpu.VMEM((MLA_TM, MLA_KVR), BF16),
                        pltpu.VMEM((MLA_TM, 128), F32)],
        compiler_params=_params(("parallel", "arbitrary")),
        name="mla_prep",
    )(p_mla, wq, wkv, gq, gkv, tq1, tq2, tk1, tk2, ctab, stab)


ATT_TQ = 256
ATT_SCALE = (MLA_NOPE + MLA_ROPE) ** -0.5


def _attend(q, k, v):
    s = lax.dot_general(q, k, (((1,), (1,)), ((), ())), preferred_element_type=F32) * ATT_SCALE
    m = jnp.max(s, axis=-1, keepdims=True)
    e = jnp.exp(s - m)
    l = jnp.sum(e, axis=-1, keepdims=True)
    return jnp.dot(e.astype(BF16), v, preferred_element_type=F32) / l


def _attention_kernel(q_ref, k_ref, v_ref, o_ref, *, with_ctx):
    qi = pl.program_id(2)
    lane = lax.broadcasted_iota(jnp.int32, (ATT_TQ, 128), 1)

    def run(lo):
        o0 = _attend(q_ref[0, 0], k_ref[0, 0, lo:, :], v_ref[0, 0, lo:, :])
        o1 = _attend(q_ref[0, 1], k_ref[0, 1, lo:, :], v_ref[0, 1, lo:, :])
        o_ref[...] = jnp.where(lane < MLA_V, pltpu.roll(o0, 64, 1), o1).astype(o_ref.dtype)

    if with_ctx:
        @pl.when(qi < LX // ATT_TQ)
        def _():
            run(0)

        @pl.when(qi == LX // ATT_TQ)
        def _():
            run(LX)
    else:
        run(0)


def attention(qh, kh, vh, with_ctx):
    nq = LX // ATT_TQ + (1 if with_ctx else 0)
    rows = NT if with_ctx else NX

    def out_row(b, qi):
        return jnp.where(qi < LX // ATT_TQ, b * (LX // ATT_TQ) + qi, NX // ATT_TQ + b)

    kv_spec = lambda: pl.BlockSpec((1, 2, LKV, 128), lambda b, p, qi: (b, p, 0, 0))
    return pl.pallas_call(
        functools.partial(_attention_kernel, with_ctx=with_ctx),
        grid=(NB, MLA_H // 2, nq),
        in_specs=[pl.BlockSpec((1, 2, ATT_TQ, 128), lambda b, p, qi: (b, p, qi, 0)), kv_spec(), kv_spec()],
        out_specs=pl.BlockSpec((ATT_TQ, 128), lambda b, p, qi: (out_row(b, qi), p)),
        out_shape=jax.ShapeDtypeStruct((rows, MLA_H * MLA_V), BF16),
        compiler_params=_params(("parallel", "parallel", "arbitrary")),
        name="mla_attention",
    )(qh, kh, vh)


S5_RX = NX // S5_T
S5_RC = NC // S5_T
S5_R = S5_RX + S5_RC
S5_W = S5_T * 128
S5_ST = 2 * S5_GB * S5_P


def _s5_scan_rows(s_scr, xin_scr, a_re, a_im, reverse):
    half = S5_ST // 2
    cx = LX // S5_T
    cc = LC // S5_T
    zero = jnp.zeros((1, half), F32)

    def sweep(base_of, n, state):
        def body(i, st):
            c = (n - 1 - i) if reverse else i
            new = []
            for b in range(NB):
                r = base_of(b) + c
                xr, xi = st[b]
                xin_scr[pl.ds(r, 1), :] = jnp.concatenate([xr, xi], axis=1)
                srow = s_scr[pl.ds(r, 1), :]
                new.append((a_re * xr - a_im * xi + srow[:, :half],
                            a_re * xi + a_im * xr + srow[:, half:]))
            return tuple(new)
        return lax.fori_loop(0, n, body, state)

    state = tuple((zero, zero) for _ in range(NB))
    state = sweep(lambda b: S5_RX + b * cc, cc, state)
    sweep(lambda b: b * cx, cx, state)


def _s5_kernel(u_ref, tiles_ref, win_ref, wout_ref, at_ref, y_ref, m_scr, s_scr, xin_scr):
    d = pl.program_id(1)
    ucat = jnp.concatenate([u_ref[pl.ds(t, S5_R, stride=S5_T), :] for t in range(S5_T)],
                           axis=1).astype(BF16)
    s_scr[...] = jnp.dot(ucat, win_ref[0, 0], preferred_element_type=F32)
    a_re = at_ref[0, 0][:, :S5_ST // 2]
    a_im = at_ref[0, 0][:, S5_ST // 2:]
    zero_tile = jnp.zeros((128, 128), BF16)

    def build(reverse):
        for tp in range(S5_T):
            for t in range(S5_T):
                lag = (tp - t) if reverse else (t - tp)
                m_scr[tp * 128:(tp + 1) * 128, t * 128:(t + 1) * 128] = (
                    tiles_ref[0, 0, lag] if lag >= 0 else zero_tile)

    @pl.when(d == 0)
    def _():
        build(False)
        _s5_scan_rows(s_scr, xin_scr, a_re, a_im, False)

    @pl.when(d == 1)
    def _():
        build(True)
        _s5_scan_rows(s_scr, xin_scr, a_re, a_im, True)

    ycat = (jnp.dot(ucat, m_scr[...], preferred_element_type=F32)
            + jnp.dot(xin_scr[...].astype(BF16), wout_ref[0, 0], preferred_element_type=F32))

    @pl.when(d == 0)
    def _():
        for t in range(S5_T):
            y_ref[pl.ds(t, S5_R, stride=S5_T), :] = ycat[:, t * 128:(t + 1) * 128]

    @pl.when(d == 1)
    def _():
        for t in range(S5_T):
            y_ref[pl.ds(t, S5_R, stride=S5_T), :] += ycat[:, t * 128:(t + 1) * 128]


def s5_conv(u, tiles, win, wout, at):
    nblk = S5_INNER // 128
    return pl.pallas_call(
        _s5_kernel,
        grid=(nblk, 2),
        in_specs=[pl.BlockSpec((NT, 128), lambda g, d: (0, g)),
                  pl.BlockSpec((1, 1, S5_T, 128, 128), lambda g, d: (d, g, 0, 0, 0)),
                  pl.BlockSpec((1, 1, S5_W, S5_ST), lambda g, d: (d, g, 0, 0)),
                  pl.BlockSpec((1, 1, S5_ST, S5_W), lambda g, d: (d, g, 0, 0)),
                  pl.BlockSpec((1, 1, 1, S5_ST), lambda g, d: (d, g, 0, 0))],
        out_specs=pl.BlockSpec((NT, 128), lambda g, d: (0, g)),
        out_shape=jax.ShapeDtypeStruct((NT, S5_INNER), F32),
        scratch_shapes=[pltpu.VMEM((S5_W, S5_W), BF16), pltpu.VMEM((S5_R, S5_ST), F32),
                        pltpu.VMEM((S5_R, S5_ST), F32)],
        compiler_params=_params(("parallel", "arbitrary")),
        name="s5_conv",
    )(u, tiles, win, wout, at)


def s5_operators(lam_re, lam_im, log_dt, b_re, b_im, c_re, c_im):
    t_len = S5_T
    dt = jnp.exp(log_dt)[..., None]
    mag = jnp.exp(lam_re * dt)
    ab_re = mag * jnp.cos(lam_im * dt)
    ab_im = mag * jnp.sin(lam_im * dt)
    den = lam_re * lam_re + lam_im * lam_im
    nr = ab_re - 1.0
    f_re = ((nr * lam_re + ab_im * lam_im) / den)[..., None]
    f_im = ((ab_im * lam_re - nr * lam_im) / den)[..., None]
    bb_re = f_re * b_re - f_im * b_im
    bb_im = f_re * b_im + f_im * b_re
    pr, pi = [jnp.ones_like(ab_re)], [jnp.zeros_like(ab_re)]
    for _ in range(t_len):
        pr, pi = pr + [pr[-1] * ab_re - pi[-1] * ab_im], pi + [pr[-1] * ab_im + pi[-1] * ab_re]
    pw_re = jnp.stack(pr)
    pw_im = jnp.stack(pi)
    cp_re = c_re[None] * pw_re[:, :, :, None, :] - c_im[None] * pw_im[:, :, :, None, :]
    cp_im = c_re[None] * pw_im[:, :, :, None, :] + c_im[None] * pw_re[:, :, :, None, :]
    kj = (jnp.einsum("jdgip,dgps->jdgis", cp_re[:t_len], bb_re, precision=HIGHEST)
          - jnp.einsum("jdgip,dgps->jdgis", cp_im[:t_len], bb_im, precision=HIGHEST))
    eye = jnp.eye(S5_GB, dtype=F32)
    nb = S5_GROUPS // S5_GB
    kj = kj.reshape(t_len, 2, nb, S5_GB, S5_S, S5_S)
    tiles = jnp.einsum("jdblis,lm->dbjlsmi", kj, eye).reshape(2, nb, t_len, 128, 128)
    ab_r = pw_re[:t_len, :, :, :, None] * bb_re[None] - pw_im[:t_len, :, :, :, None] * bb_im[None]
    ab_i = pw_re[:t_len, :, :, :, None] * bb_im[None] + pw_im[:t_len, :, :, :, None] * bb_re[None]
    ab = jnp.stack([ab_r, ab_i], axis=-1)
    ab = jnp.stack([ab[::-1, 0], ab[:, 1]], axis=1)
    ab = ab.reshape(t_len, 2, nb, S5_GB, S5_P, S5_S, 2)
    win = jnp.einsum("tdblpsc,lm->dbtlscmp", ab, eye).reshape(2, nb, t_len * 128, S5_ST)
    q_re = jnp.stack([cp_re[1:, 0], cp_re[:0:-1, 1]], axis=1)
    q_im = jnp.stack([cp_im[1:, 0], cp_im[:0:-1, 1]], axis=1)
    qq = jnp.stack([q_re, -q_im], axis=-1)
    qq = qq.reshape(t_len, 2, nb, S5_GB, S5_S, S5_P, 2)
    wout = jnp.einsum("tdblipc,lm->dbclptmi", qq, eye).reshape(2, nb, S5_ST, t_len * 128)
    at = jnp.stack([pw_re[t_len], pw_im[t_len]], axis=1)
    at = at.reshape(2, 2, nb, S5_GB * S5_P).transpose(0, 2, 1, 3).reshape(2, nb, 1, S5_ST)
    return tiles.astype(BF16), win.astype(BF16), wout.astype(BF16), at


def _s5_finish_kernel(y_ref, u_ref, d_ref, w_ref, o_ref):
    y = y_ref[...] + d_ref[...] * u_ref[...]
    v = jax.nn.gelu(y)
    gate = jnp.dot(v.astype(BF16), w_ref[...], preferred_element_type=F32)
    o_ref[...] = (v * jax.nn.sigmoid(gate)).astype(o_ref.dtype)


def s5_finish(y, u, d_skip, w_glu, rows):
    tm = 512
    blk = lambda: pl.BlockSpec((tm, S5_INNER), lambda i: (i, 0))
    return pl.pallas_call(
        _s5_finish_kernel,
        grid=(rows // tm,),
        in_specs=[blk(), blk(), pl.BlockSpec((1, S5_INNER), lambda i: (0, 0)),
                  pl.BlockSpec((S5_INNER, S5_INNER), lambda i: (0, 0))],
        out_specs=blk(),
        out_shape=jax.ShapeDtypeStruct((rows, S5_INNER), BF16),
        compiler_params=_params(("parallel",)),
        name="s5_finish",
    )(y, u, d_skip.reshape(1, S5_INNER), w_glu)


def _merge_kernel(ssd_ref, mla_ref, s5_ref, g1_ref, g2_ref, g3_ref, w1_ref, w2_ref, w3_ref, o_ref):
    m = (g1_ref[...] * jnp.dot(ssd_ref[...], w1_ref[...], preferred_element_type=F32)
         + g2_ref[...] * jnp.dot(mla_ref[...], w2_ref[...], preferred_element_type=F32)
         + g3_ref[...] * jnp.dot(s5_ref[...], w3_ref[...], preferred_element_type=F32))
    o_ref[...] = m.astype(o_ref.dtype)


def merge(ssd_o, mla_o, s5_o, gates, w1, w2, w3, rows):
    tm, tn = 512, 512
    nj = D // tn
    act = lambda: pl.BlockSpec((tm, 1024), lambda i, j: (i, 0))
    wsp = lambda: pl.BlockSpec((1024, tn), lambda i, j: (0, j))
    gsp = lambda k: pl.BlockSpec((tm, tn), lambda i, j: (i, k * nj + j))
    return pl.pallas_call(
        _merge_kernel,
        grid=(rows // tm, nj),
        in_specs=[act(), act(), act(), gsp(0), gsp(1), gsp(2), wsp(), wsp(), wsp()],
        out_specs=pl.BlockSpec((tm, tn), lambda i, j: (i, j)),
        out_shape=jax.ShapeDtypeStruct((rows, D), BF16),
        compiler_params=_params(("parallel", "parallel")),
        name="merge",
    )(ssd_o, mla_o, s5_o, gates, gates, gates, w1, w2, w3)


def _outproj_kernel(m_ref, w_ref, x_ref, g_ref, o_ref):
    o_ref[...] = x_ref[...] + g_ref[0] * jnp.dot(m_ref[...], w_ref[...], preferred_element_type=F32)


def outproj(m, w_out, x, modtab, k_gate, rows):
    tm, tn = 512, 512
    tpb = LX // tm
    return pl.pallas_call(
        _outproj_kernel,
        grid=(rows // tm, D // tn),
        in_specs=[pl.BlockSpec((tm, D), lambda i, j: (i, 0)),
                  pl.BlockSpec((D, tn), lambda i, j: (0, j)),
                  pl.BlockSpec((tm, tn), lambda i, j: (i, j)),
                  pl.BlockSpec((1, 1, tn), lambda i, j: (_mod_row(i, tpb) * 6 + k_gate, 0, j))],
        out_specs=pl.BlockSpec((tm, tn), lambda i, j: (i, j)),
        out_shape=jax.ShapeDtypeStruct((rows, D), F32),
        compiler_params=_params(("parallel", "parallel")),
        name="outproj",
    )(m, w_out, x, modtab)


def _router_kernel(x_ref, g_ref, sc_ref, sh_ref, w_ref, h_ref, o_ref):
    x = x_ref[...]
    ms = jnp.mean(x * x, axis=-1, keepdims=True)
    h = x * lax.rsqrt(ms + EPS) * g_ref[...] * (1.0 + sc_ref[0]) + sh_ref[0]
    h_ref[...] = h.astype(BF16)
    logits = lax.dot_general(w_ref[...], h, (((1,), (1,)), ((), ())),
                             precision=HIGHEST, preferred_element_type=F32)
    m = jnp.max(logits, axis=0, keepdims=True)
    e = jnp.exp(logits - m)
    o_ref[...] = e / jnp.sum(e, axis=0, keepdims=True)


def router(x, gain, modtab, w_router_t, rows):
    tm = 256
    tpb = LX // tm
    return pl.pallas_call(
        _router_kernel,
        grid=(rows // tm,),
        in_specs=[pl.BlockSpec((tm, D), lambda i: (i, 0)),
                  pl.BlockSpec((1, D), lambda i: (0, 0)),
                  pl.BlockSpec((1, 1, D), lambda i: (_mod_row(i, tpb) * 6 + 4, 0, 0)),
                  pl.BlockSpec((1, 1, D), lambda i: (_mod_row(i, tpb) * 6 + 3, 0, 0)),
                  pl.BlockSpec((N_EXP, D), lambda i: (0, 0))],
        out_specs=[pl.BlockSpec((tm, D), lambda i: (i, 0)),
                   pl.BlockSpec((N_EXP, tm), lambda i: (0, i))],
        out_shape=[jax.ShapeDtypeStruct((rows, D), BF16), jax.ShapeDtypeStruct((N_EXP, rows), F32)],
        compiler_params=_params(("parallel",)),
        name="moe_router",
    )(x, gain.reshape(1, D), modtab, modtab, w_router_t)


def _prefix_count(mask_f, n):
    row = lax.broadcasted_iota(jnp.int32, (128, 128), 0)
    col = lax.broadcasted_iota(jnp.int32, (128, 128), 1)
    upper = (row < col).astype(BF16)
    offset = jnp.zeros((N_EXP, 1), F32)
    parts = []
    for j in range(n // 128):
        blk = mask_f[:, j * 128:(j + 1) * 128]
        parts.append(jnp.dot(blk.astype(BF16), upper, preferred_element_type=F32) + offset)
        offset = offset + jnp.sum(blk, axis=1, keepdims=True)
    return jnp.concatenate(parts, axis=1)


def _select_kernel(aff_ref, g_ref, pos_ref, *, n, cap):
    aff = aff_ref[...]
    bits = lax.bitcast_convert_type(aff, jnp.int32)

    def body(i, thr):
        cand = jnp.bitwise_or(thr, jnp.left_shift(jnp.int32(1), 30 - i))
        cnt = jnp.sum((bits >= cand).astype(F32), axis=1, keepdims=True)
        return jnp.where(cnt >= cap, cand, thr)

    thr = lax.fori_loop(0, 31, body, jnp.zeros((N_EXP, 1), jnp.int32))
    gt = bits > thr
    eq = (bits == thr).astype(F32)
    need = cap - jnp.sum(gt.astype(F32), axis=1, keepdims=True)
    sel = jnp.logical_or(gt, jnp.logical_and(eq > 0, _prefix_count(eq, n) < need))
    sel_f = sel.astype(F32)
    g_ref[...] = jnp.where(sel, aff, 0.0)
    pos_ref[...] = jnp.where(sel, _prefix_count(sel_f, n), -1.0)


def select(aff_t, n, cap, row0):
    b0 = row0 // n
    spec = lambda: pl.BlockSpec((N_EXP, n), lambda b: (0, b))
    shp = jax.ShapeDtypeStruct((N_EXP, NB * n), F32)
    return pl.pallas_call(
        functools.partial(_select_kernel, n=n, cap=cap),
        grid=(NB,),
        in_specs=[pl.BlockSpec((N_EXP, n), lambda b: (0, b0 + b))],
        out_specs=[spec(), spec()],
        out_shape=[shp, shp],
        compiler_params=_params(("parallel",)),
        name="moe_select",
    )(aff_t)


def _gather_kernel(h_ref, pos_ref, g_ref, xs_ref, gs_ref, *, n, cap):
    pos = pos_ref[0, 0]
    slot = lax.broadcasted_iota(jnp.int32, (cap, n), 0).astype(F32)
    onehot = slot == pos
    xs_ref[0] = jnp.dot(onehot.astype(BF16), h_ref[...], preferred_element_type=F32).astype(BF16)
    gsel = jnp.sum(jnp.where(onehot, g_ref[0, 0], 0.0), axis=1, keepdims=True)
    gs_ref[0] = jnp.broadcast_to(gsel, (cap, 128))


def gather(h2, pos_t, g_t, n, cap, row0):
    b0 = row0 // n
    tab = lambda: pl.BlockSpec((1, 1, 1, n), lambda b, e: (e, b, 0, 0))
    return pl.pallas_call(
        functools.partial(_gather_kernel, n=n, cap=cap),
        grid=(NB, N_EXP),
        in_specs=[pl.BlockSpec((n, D), lambda b, e: (b0 + b, 0)), tab(), tab()],
        out_specs=[pl.BlockSpec((1, cap, D), lambda b, e: (e, b, 0)),
                   pl.BlockSpec((1, cap, 128), lambda b, e: (e, b, 0))],
        out_shape=[jax.ShapeDtypeStruct((N_EXP, NB * cap, D), BF16),
                   jax.ShapeDtypeStruct((N_EXP, NB * cap, 128), F32)],
        compiler_params=_params(("parallel", "arbitrary")),
        name="moe_gather",
    )(h2, pos_t, g_t)


EXP_TF = 256


def _expert_kernel(*refs, with_ctx):
    if with_ctx:
        xs_ref, xc_ref, gs_ref, gc_ref, wg_ref, wu_ref, wd_ref, ys_ref, yc_ref, acc_ref = refs
        xs = jnp.concatenate([xs_ref[0], xc_ref[0]], axis=0)
    else:
        xs_ref, gs_ref, wg_ref, wu_ref, wd_ref, ys_ref, acc_ref = refs
        xs = xs_ref[0]
    f = pl.program_id(1)
    hg = jnp.dot(xs, wg_ref[0].astype(BF16), preferred_element_type=F32)
    hu = jnp.dot(xs, wu_ref[0].astype(BF16), preferred_element_type=F32)
    hid = (hg * jax.nn.sigmoid(hg) * hu).astype(BF16)
    contrib = jnp.dot(hid, wd_ref[0].astype(BF16), preferred_element_type=F32)

    @pl.when(f == 0)
    def _():
        acc_ref[...] = contrib

    @pl.when(f > 0)
    def _():
        acc_ref[...] += contrib

    @pl.when(f == FF // EXP_TF - 1)
    def _():
        rx = NB * CAP_X
        ys_ref[0] = (acc_ref[0:rx, :] * gs_ref[0][:, 0:1]).astype(BF16)
        if with_ctx:
            yc_ref[0] = (acc_ref[rx:, :] * gc_ref[0][:, 0:1]).astype(BF16)


def experts(xs, gs, w_gate, w_up, w_down, xc=None, gc=None):
    with_ctx = xc is not None
    rx, rc = NB * CAP_X, NB * CAP_C
    tok = lambda r, w: pl.BlockSpec((1, r, w), lambda e, f: (e, 0, 0))
    in_specs = [tok(rx, D)] + ([tok(rc, D)] if with_ctx else []) + [tok(rx, 128)] + ([tok(rc, 128)] if with_ctx else [])
    in_specs += [pl.BlockSpec((1, D, EXP_TF), lambda e, f: (e, 0, f)),
                 pl.BlockSpec((1, D, EXP_TF), lambda e, f: (e, 0, f)),
                 pl.BlockSpec((1, EXP_TF, D), lambda e, f: (e, f, 0))]
    out_specs = [tok(rx, D)] + ([tok(rc, D)] if with_ctx else [])
    out_shape = [jax.ShapeDtypeStruct((N_EXP, rx, D), BF16)] + (
        [jax.ShapeDtypeStruct((N_EXP, rc, D), BF16)] if with_ctx else [])
    args = [xs] + ([xc] if with_ctx else []) + [gs] + ([gc] if with_ctx else []) + [w_gate, w_up, w_down]
    return pl.pallas_call(
        functools.partial(_expert_kernel, with_ctx=with_ctx),
        grid=(N_EXP, FF // EXP_TF),
        in_specs=in_specs,
        out_specs=out_specs,
        out_shape=out_shape,
        scratch_shapes=[pltpu.VMEM((rx + (rc if with_ctx else 0), D), F32)],
        compiler_params=_params(("parallel", "arbitrary")),
        name="moe_experts",
    )(*args)


def _combine_kernel(pos_ref, ys_ref, x_ref, g_ref, o_ref, acc_ref, *, tm, cap):
    e = pl.program_id(2)
    cap_pad = max(cap, 128)
    slot = lax.broadcasted_iota(jnp.int32, (cap_pad, tm), 0).astype(F32)
    onehot_t = (slot == pos_ref[0, 0]).astype(F32).T.astype(BF16)
    ys = ys_ref[0]
    if cap_pad > cap:
        ys = jnp.concatenate([ys, jnp.zeros((cap_pad - cap, D), BF16)], axis=0)
    contrib = jnp.dot(onehot_t, ys, preferred_element_type=F32)

    @pl.when(e == 0)
    def _():
        acc_ref[...] = contrib

    @pl.when(e > 0)
    def _():
        acc_ref[...] += contrib

    @pl.when(e == N_EXP - 1)
    def _():
        o_ref[...] = x_ref[...] + g_ref[0] * acc_ref[...]


def combine(pos_t, ys, x, modtab, k_gate, n, cap, row0, mod_row_of_batch):
    tm = min(n, 512)
    tpb = n // tm
    r0 = row0 // tm
    rows = NB * n
    return pl.pallas_call(
        functools.partial(_combine_kernel, tm=tm, cap=cap),
        grid=(NB, tpb, N_EXP),
        in_specs=[pl.BlockSpec((1, 1, 1, tm), lambda b, t, e: (e, b, 0, t)),
                  pl.BlockSpec((1, cap, D), lambda b, t, e: (e, b, 0)),
                  pl.BlockSpec((tm, D), lambda b, t, e: (r0 + b * tpb + t, 0)),
                  pl.BlockSpec((1, 1, D), lambda b, t, e: (mod_row_of_batch(b) * 6 + k_gate, 0, 0))],
        out_specs=pl.BlockSpec((tm, D), lambda b, t, e: (b * tpb + t, 0)),
        out_shape=jax.ShapeDtypeStruct((rows, D), F32),
        scratch_shapes=[pltpu.VMEM((tm, D), F32)],
        compiler_params=_params(("parallel", "parallel", "arbitrary")),
        name="moe_combine",
    )(pos_t, ys, x, modtab)


def moe_layer(xa, gain, modtab, w_router_t, w_gate, w_up, w_down, with_ctx):
    h2, aff = router(xa, gain, modtab, w_router_t, NT if with_ctx else NX)

    def route(n, cap, row0):
        g_t, pos_t = select(aff, n, cap, row0)
        g_t = g_t.reshape(N_EXP, NB, 1, n)
        pos_t = pos_t.reshape(N_EXP, NB, 1, n)
        xs, gs = gather(h2, pos_t, g_t, n, cap, row0)
        return pos_t, xs, gs

    pos_x, xs_x, gs_x = route(LX, CAP_X, 0)
    if with_ctx:
        pos_c, xs_c, gs_c = route(LC, CAP_C, NX)
        ys_x, ys_c = experts(xs_x, gs_x, w_gate, w_up, w_down, xs_c, gs_c)
    else:
        (ys_x,) = experts(xs_x, gs_x, w_gate, w_up, w_down)
    out_x = combine(pos_x, ys_x, xa, modtab, 5, LX, CAP_X, 0, lambda b: b)
    if not with_ctx:
        return out_x
    out_c = combine(pos_c, ys_c, xa, modtab, 5, LC, CAP_C, NX, lambda b: NB)
    return jnp.concatenate([out_x, out_c], axis=0)


def _rope_tables():
    n_freq = MLA_ROPE // 4
    inv = ROPE_BASE ** (-jnp.arange(n_freq, dtype=F32) / n_freq)
    rows = jnp.repeat(jnp.arange(LX // GRID_W, dtype=F32), GRID_W)
    cols = jnp.tile(jnp.arange(GRID_W, dtype=F32), LX // GRID_W)
    ang = jnp.concatenate([rows[:, None] * inv, cols[:, None] * inv], axis=-1)
    cos, sin = jnp.cos(ang), jnp.sin(ang)
    one = jnp.ones((LX, MLA_NOPE), F32)
    zero = jnp.zeros((LX, 32), F32)
    ct_x = jnp.concatenate([one, cos, cos, zero], axis=1)
    st_x = jnp.concatenate([0 * one, -sin, sin, zero], axis=1)
    ct_c = jnp.concatenate([jnp.ones((NC, 96), F32), jnp.zeros((NC, 32), F32)], axis=1)
    ctab = jnp.concatenate([jnp.tile(ct_x, (NB, 1)), ct_c], axis=0)
    stab = jnp.concatenate([jnp.tile(st_x, (NB, 1)), jnp.zeros((NC, 128), F32)], axis=0)
    return ctab, stab


_PERM = list(range(0, MLA_ROPE, 2)) + list(range(1, MLA_ROPE, 2))
_PERM_SW = list(range(1, MLA_ROPE, 2)) + list(range(0, MLA_ROPE, 2))


def _lane_table(first64, rope32):
    return jnp.concatenate([first64, rope32, jnp.zeros((32,), F32)]).reshape(1, 128)


def _layer_weights(i, w_in, ssd_a_log, ssd_dt_bias, mla_w_q_b, mla_w_kv_b, mla_q_gain, mla_k_gain):
    w = w_in[i]
    wdt = w[:, SSD_INNER + SSD_CONV_CH:O1].reshape(D, SSD_G, SSD_HPG)
    wdt_col = jnp.pad(wdt, ((0, 0), (0, 0), (0, 128 - SSD_HPG))).reshape(D, SSD_G * 128)
    w_ssd = jnp.concatenate([w[:, :SSD_INNER + SSD_CONV_CH], wdt_col], axis=1).astype(BF16)
    wdt_row = jnp.pad(jnp.transpose(wdt, (1, 2, 0)), ((0, 0), (0, 8 - SSD_HPG), (0, 0))).reshape(SSD_G * 8, D)
    kpe = w[:, O1 + MLA_QR + MLA_KVR:O2]
    kpe128 = jnp.concatenate([jnp.zeros((D, MLA_NOPE), F32), kpe[:, _PERM], kpe[:, _PERM_SW]], axis=1)
    w_mla = jnp.concatenate([w[:, O1:O1 + MLA_QR + MLA_KVR], kpe128], axis=1).astype(BF16)
    w_s5 = w[:, O2:O3].astype(BF16)
    w_gate = w[:, O3:].astype(BF16)
    par = jnp.stack([ssd_a_log[i, 0], ssd_a_log[i, 1], ssd_dt_bias[i, 0], ssd_dt_bias[i, 1]])
    par = par.reshape(4, SSD_G, SSD_HPG).transpose(1, 0, 2)
    pcol = jnp.pad(par, ((0, 0), (0, 4), (0, 128 - SSD_HPG)))
    prow = jnp.broadcast_to(jnp.pad(par, ((0, 0), (0, 0), (0, 8 - SSD_HPG)))[..., None], (SSD_G, 4, 8, 128))
    wq = mla_w_q_b[i].reshape(MLA_QR, MLA_H, MLA_NOPE + MLA_ROPE)
    wq = jnp.concatenate([wq[..., :MLA_NOPE], wq[..., MLA_NOPE:][..., _PERM], wq[..., MLA_NOPE:][..., _PERM_SW]],
                         axis=-1).transpose(1, 0, 2).astype(BF16)
    wkv = mla_w_kv_b[i].reshape(MLA_KVR, MLA_H, 128).transpose(1, 0, 2).astype(BF16)
    qg, kg = mla_q_gain[i], mla_k_gain[i]
    tq1 = _lane_table(qg[:MLA_NOPE], qg[MLA_NOPE:][jnp.array(_PERM)])
    tq2 = _lane_table(jnp.zeros((MLA_NOPE,), F32), qg[MLA_NOPE:][jnp.array(_PERM_SW)])
    tk1 = jnp.concatenate([_lane_table(kg[:MLA_NOPE], jnp.zeros((MLA_ROPE,), F32)),
                           _lane_table(jnp.zeros((MLA_NOPE,), F32), kg[MLA_NOPE:][jnp.array(_PERM)])], axis=0)
    tk2 = _lane_table(jnp.zeros((MLA_NOPE,), F32), kg[MLA_NOPE:][jnp.array(_PERM_SW)])
    return dict(w_ssd=w_ssd, wdt_row=wdt_row.astype(BF16), w_mla=w_mla, w_s5=w_s5, w_gate=w_gate,
                pcol=pcol, prow=prow, wq=wq, wkv=wkv, tq1=tq1, tq2=tq2, tk1=tk1, tk2=tk2)


def kernel(x, c, ctx, c_ctx, norm1_gain, norm2_gain, w_mod, b_mod, w_in, ssd_conv_w, ssd_conv_b, ssd_a_log, ssd_dt_bias, ssd_d, ssd_norm_gain, mla_q_a_gain, mla_kv_a_gain, mla_w_q_b, mla_w_kv_b, mla_q_gain, mla_k_gain, s5_lam_re, s5_lam_im, s5_log_dt, s5_b_re, s5_b_im, s5_c_re, s5_c_im, s5_d, s5_w_glu, w_branch_ssd, w_branch_mla, w_branch_s5, w_out, moe_router, moe_w_gate, moe_w_up, moe_w_down):
    ctab, stab = _rope_tables()
    cvec = jnp.concatenate([c, c_ctx[None], jnp.zeros((8 - NB - 1, D), F32)], axis=0)
    mod = modulation(cvec, w_mod, b_mod)
    xa = jnp.concatenate([x.reshape(NX, D), ctx.reshape(NC, D)], axis=0)
    for i in range(DEPTH):
        need_ctx = i < DEPTH - 1
        rows = NT if need_ctx else NX
        modtab = mod[i].reshape(8 * 6, 1, D)
        lw = _layer_weights(i, w_in, ssd_a_log, ssd_dt_bias, mla_w_q_b, mla_w_kv_b, mla_q_gain, mla_k_gain)
        h = normmod(xa, norm1_gain[i], modtab, 0, 1, NT)
        p_ssd = matmul(h, lw["w_ssd"], 1024, 512, name="inproj_ssd")
        dt_rows = matmul_nt(lw["wdt_row"], h, 1024)
        p_mla = matmul(h, lw["w_mla"], 1024, 896, name="inproj_mla")
        u_s5 = matmul(h, lw["w_s5"], 1024, 512, name="inproj_s5")
        gates = matmul(h, lw["w_gate"], 1024, 1024, act="sigmoid", name="inproj_gate", rows=rows)
        xbc = ssd_conv(p_ssd, ssd_conv_w[i], ssd_conv_b[i])
        yf, yr = ssd_scan(xbc, p_ssd, dt_rows, lw["pcol"], lw["prow"])
        dsum = jnp.repeat(ssd_d[i, 0] + ssd_d[i, 1], SSD_P).reshape(1, SSD_INNER)
        ssd_o = ssd_finish(yf, yr, xbc, p_ssd, dsum, ssd_norm_gain[i], rows)
        qh, kh, vh = mla_prep(p_mla, lw["wq"], lw["wkv"], mla_q_a_gain[i].reshape(1, MLA_QR),
                              mla_kv_a_gain[i].reshape(1, MLA_KVR), lw["tq1"], lw["tq2"], lw["tk1"], lw["tk2"],
                              ctab, stab)
        mla_o = attention(qh, kh, vh, need_ctx)
        ops = s5_operators(s5_lam_re[i], s5_lam_im[i], s5_log_dt[i], s5_b_re[i], s5_b_im[i], s5_c_re[i], s5_c_im[i])
        y_s5 = s5_conv(u_s5, *ops)
        s5_o = s5_finish(y_s5, u_s5, s5_d[i], s5_w_glu[i].astype(BF16), rows)
        m = merge(ssd_o, mla_o, s5_o, gates, w_branch_ssd[i].astype(BF16), w_branch_mla[i].astype(BF16),
                  w_branch_s5[i].astype(BF16), rows)
        xa = outproj(m, w_out[i].astype(BF16), xa, modtab, 2, rows)
        xa = moe_layer(xa, norm2_gain[i], modtab, moe_router[i].T, moe_w_gate[i], moe_w_up[i], moe_w_down[i],
                       need_ctx)
    return xa[:NX].reshape(NB, LX, D)
```

```python
import functools
import math

import jax
import jax.numpy as jnp
from jax import lax
from jax.experimental import pallas as pl
from jax.experimental.pallas import tpu as pltpu

F32 = jnp.float32
BF16 = jnp.bfloat16
HIGHEST = lax.Precision.HIGHEST

D = 2048
NB = 4
LX = 2048
LC = 256
NX = NB * LX
NC = NB * LC
NT = NX + NC
DEPTH = 2
EPS = 1e-6
GRID_W = 64

SSD_HEADS = 16
SSD_P = 64
SSD_INNER = 1024
SSD_G = 4
SSD_HPG = 4
SSD_N = 128
SSD_Q = 128
SSD_BC = SSD_G * SSD_N
SSD_CONV_CH = SSD_INNER + 2 * SSD_BC
SSD_IN = SSD_INNER + SSD_CONV_CH + SSD_HEADS

MLA_H = 16
MLA_QR = 512
MLA_KVR = 256
MLA_NOPE = 64
MLA_ROPE = 32
MLA_V = 64
MLA_IN = MLA_QR + MLA_KVR + MLA_ROPE
ROPE_BASE = 10000.0

S5_INNER = 1024
S5_S = 16
S5_GROUPS = 64
S5_P = 64
S5_T = 8
S5_GB = 8

N_EXP = 16
FF = 2048
CAP_X = 2 * LX // N_EXP
CAP_C = 2 * LC // N_EXP

O1 = SSD_IN
O2 = O1 + MLA_IN
O3 = O2 + S5_INNER

VMEM_LIMIT = 56 * 1024 * 1024


def _params(sem):
    return pltpu.CompilerParams(dimension_semantics=sem, vmem_limit_bytes=VMEM_LIMIT)


def _mod_row(i, tiles_per_batch):
    return jnp.minimum(i // tiles_per_batch, NB)


def _modulation_kernel(c_ref, w_ref, b_ref, o_ref):
    c = c_ref[...]
    a = (c * jax.nn.sigmoid(c)).astype(BF16)
    o_ref[0] = jnp.dot(a, w_ref[0].astype(BF16), preferred_element_type=F32) + b_ref[0]


def modulation(cvec, w_mod, b_mod):
    tn = 1536
    n = w_mod.shape[-1]
    return pl.pallas_call(
        _modulation_kernel,
        grid=(DEPTH, n // tn),
        in_specs=[pl.BlockSpec((8, D), lambda l, j: (0, 0)),
                  pl.BlockSpec((1, D, tn), lambda l, j: (l, 0, j)),
                  pl.BlockSpec((1, 1, tn), lambda l, j: (l, 0, j))],
        out_specs=pl.BlockSpec((1, 8, tn), lambda l, j: (l, 0, j)),
        out_shape=jax.ShapeDtypeStruct((DEPTH, 8, n), F32),
        compiler_params=_params(("parallel", "parallel")),
        name="modulation",
    )(cvec, w_mod, b_mod.reshape(DEPTH, 1, n))


def _normmod_kernel(x_ref, g_ref, sc_ref, sh_ref, o_ref):
    x = x_ref[...]
    ms = jnp.mean(x * x, axis=-1, keepdims=True)
    y = x * lax.rsqrt(ms + EPS) * g_ref[...]
    o_ref[...] = (y * (1.0 + sc_ref[0]) + sh_ref[0]).astype(o_ref.dtype)


def normmod(x, gain, modtab, k_shift, k_scale, rows):
    tm = 256
    tpb = LX // tm
    return pl.pallas_call(
        _normmod_kernel,
        grid=(rows // tm,),
        in_specs=[pl.BlockSpec((tm, D), lambda i: (i, 0)),
                  pl.BlockSpec((1, D), lambda i: (0, 0)),
                  pl.BlockSpec((1, 1, D), lambda i: (_mod_row(i, tpb) * 6 + k_scale, 0, 0)),
                  pl.BlockSpec((1, 1, D), lambda i: (_mod_row(i, tpb) * 6 + k_shift, 0, 0))],
        out_specs=pl.BlockSpec((tm, D), lambda i: (i, 0)),
        out_shape=jax.ShapeDtypeStruct((rows, D), BF16),
        compiler_params=_params(("parallel",)),
        name="normmod",
    )(x, gain.reshape(1, D), modtab, modtab)


def _mm_kernel(a_ref, w_ref, o_ref, *, act):
    acc = jnp.dot(a_ref[...], w_ref[...], preferred_element_type=F32)
    if act == "sigmoid":
        acc = jax.nn.sigmoid(acc)
    o_ref[...] = acc.astype(o_ref.dtype)


def matmul(a, w, tm, tn, act=None, out_dtype=F32, name="matmul", rows=None):
    m, k = a.shape
    m = rows or m
    n = w.shape[1]
    return pl.pallas_call(
        functools.partial(_mm_kernel, act=act),
        grid=(m // tm, n // tn),
        in_specs=[pl.BlockSpec((tm, k), lambda i, j: (i, 0)),
                  pl.BlockSpec((k, tn), lambda i, j: (0, j))],
        out_specs=pl.BlockSpec((tm, tn), lambda i, j: (i, j)),
        out_shape=jax.ShapeDtypeStruct((m, n), out_dtype),
        compiler_params=_params(("parallel", "parallel")),
        name=name,
    )(a, w)


def _mm_nt_kernel(w_ref, a_ref, o_ref):
    o_ref[...] = lax.dot_general(w_ref[...], a_ref[...], (((1,), (1,)), ((), ())),
                                 preferred_element_type=F32)


def matmul_nt(w_rows, a, tm):
    r, k = w_rows.shape
    m = a.shape[0]
    return pl.pallas_call(
        _mm_nt_kernel,
        grid=(m // tm,),
        in_specs=[pl.BlockSpec((r, k), lambda i: (0, 0)),
                  pl.BlockSpec((tm, k), lambda i: (i, 0))],
        out_specs=pl.BlockSpec((r, tm), lambda i: (0, i)),
        out_shape=jax.ShapeDtypeStruct((r, m), F32),
        compiler_params=_params(("parallel",)),
        name="matmul_nt",
    )(w_rows, a)


CONV_TM = 256
CONV_TC = 512
CONV_HALO = 8


def _conv_kernel(prev_ref, cur_ref, next_ref, w_ref, b_ref, o_ref, ext_ref):
    i = pl.program_id(0)
    blocks_per_seq = LX // CONV_TM
    is_ctx = i >= NX // CONV_TM
    first = jnp.logical_or(is_ctx, i % blocks_per_seq == 0)
    last = jnp.logical_or(is_ctx, i % blocks_per_seq == blocks_per_seq - 1)
    zeros = jnp.zeros((CONV_HALO, CONV_TC), F32)
    ext_ref[0:CONV_HALO, :] = jnp.where(first, zeros, prev_ref[...])
    ext_ref[CONV_HALO:CONV_HALO + CONV_TM, :] = cur_ref[...]
    ext_ref[CONV_HALO + CONV_TM:, :] = jnp.where(last, zeros, next_ref[...])
    acc = b_ref[...] + jnp.zeros((CONV_TM, CONV_TC), F32)
    for k in range(5):
        acc = acc + ext_ref[pl.ds(CONV_HALO + k - 2, CONV_TM), :] * w_ref[k:k + 1, :]
    o_ref[...] = acc * jax.nn.sigmoid(acc)


def ssd_conv(p_ssd, conv_w, conv_b):
    nblk = NT // CONV_TM
    hb = CONV_TM // CONV_HALO
    c0 = SSD_INNER // CONV_TC
    w8 = jnp.zeros((8, SSD_CONV_CH), F32).at[:5].set(conv_w)
    return pl.pallas_call(
        _conv_kernel,
        grid=(nblk, SSD_CONV_CH // CONV_TC),
        in_specs=[pl.BlockSpec((CONV_HALO, CONV_TC), lambda i, j: (jnp.maximum(i * hb - 1, 0), c0 + j)),
                  pl.BlockSpec((CONV_TM, CONV_TC), lambda i, j: (i, c0 + j)),
                  pl.BlockSpec((CONV_HALO, CONV_TC),
                               lambda i, j: (jnp.minimum((i + 1) * hb, NT // CONV_HALO - 1), c0 + j)),
                  pl.BlockSpec((8, CONV_TC), lambda i, j: (0, j)),
                  pl.BlockSpec((1, CONV_TC), lambda i, j: (0, j))],
        out_specs=pl.BlockSpec((CONV_TM, CONV_TC), lambda i, j: (i, j)),
        out_shape=jax.ShapeDtypeStruct((NT, SSD_CONV_CH), F32),
        scratch_shapes=[pltpu.VMEM((CONV_TM + 2 * CONV_HALO, CONV_TC), F32)],
        compiler_params=_params(("parallel", "parallel")),
        name="ssd_conv",
    )(p_ssd, p_ssd, p_ssd, w8, conv_b.reshape(1, SSD_CONV_CH))


NCH_C = LC // SSD_Q
NCH_X = LX // SSD_Q
SSD_STEPS = NCH_C + NCH_X


def _softplus(x):
    return jnp.maximum(x, 0.0) + jnp.log1p(jnp.exp(-jnp.abs(x)))


def _ssd_one_direction(d, g, xs_ref, b_ref, c_ref, dtc_ref, dtr_ref, pcol, prow, st_ref, y_ref, row, col):
    q = SSD_Q
    gw = SSD_HPG * SSD_P
    xs = xs_ref[:, g * gw:(g + 1) * gw]
    bm = b_ref[:, g * SSD_N:(g + 1) * SSD_N].astype(BF16)
    cm = c_ref[:, g * SSD_N:(g + 1) * SSD_N].astype(BF16)
    tri_l = (col <= row).astype(F32)
    tri_u = (col >= row).astype(F32)
    t_col = tri_l if d == 0 else tri_u
    t_row = tri_u if d == 0 else tri_l
    mask = (col <= row) if d == 0 else (col >= row)
    dt_c = _softplus(dtc_ref[:, g * 128:(g + 1) * 128] + pcol[2 + d:3 + d, :])
    a_c = dt_c * (-jnp.exp(pcol[d:d + 1, :]))
    cs_c = jnp.dot(t_col, a_c, precision=HIGHEST, preferred_element_type=F32)
    dt_r = _softplus(dtr_ref[g * 8:(g + 1) * 8, :] + prow[2 + d])
    a_r = dt_r * (-jnp.exp(prow[d]))
    cs_r = jnp.dot(a_r, t_row, precision=HIGHEST, preferred_element_type=F32)
    edge = q - 1 if d == 0 else 0
    tot_c = cs_c[edge:edge + 1, :]
    tot_r = cs_r[:, edge:edge + 1]

    def lanes(v):
        return jnp.concatenate([jnp.broadcast_to(v[:, h:h + 1], (v.shape[0], SSD_P))
                                for h in range(SSD_HPG)], axis=1)

    dt_b = lanes(dt_c)
    cs_b = lanes(cs_c)
    tot_b = lanes(tot_c)
    xdt = xs * dt_b
    scores = lax.dot_general(cm, bm, (((1,), (1,)), ((), ())), preferred_element_type=F32)
    st = st_ref[d, g]
    y_off = lax.dot_general(cm, st.astype(BF16), (((1,), (1,)), ((), ())),
                            preferred_element_type=F32) * jnp.exp(cs_b)
    xdt16 = xdt.astype(BF16)
    ys = []
    for h in range(SSD_HPG):
        diff = cs_c[:, h:h + 1] - cs_r[h:h + 1, :]
        decay = jnp.exp(jnp.where(mask, diff, -jnp.inf))
        ys.append(jnp.dot((scores * decay).astype(BF16), xdt16[:, h * SSD_P:(h + 1) * SSD_P],
                          preferred_element_type=F32))
    y_ref[:, g * gw:(g + 1) * gw] = jnp.concatenate(ys, axis=1) + y_off
    xw_t = (xdt * jnp.exp(tot_b - cs_b)).T.astype(BF16)
    s_new = jnp.dot(xw_t, bm, preferred_element_type=F32)
    dec = jnp.concatenate([jnp.broadcast_to(jnp.exp(tot_r[h:h + 1, :]), (SSD_P, SSD_N))
                           for h in range(SSD_HPG)], axis=0)
    st_ref[d, g] = st * dec + s_new


def _ssd_kernel(xf, bf, cf, dtcf, dtrf, xr, br, cr, dtcr, dtrr, pcol_ref, prow_ref, yf_ref, yr_ref, st_ref):
    k = pl.program_id(1)

    @pl.when(k == 0)
    def _():
        st_ref[...] = jnp.zeros_like(st_ref)

    row = lax.broadcasted_iota(jnp.int32, (SSD_Q, SSD_Q), 0)
    col = lax.broadcasted_iota(jnp.int32, (SSD_Q, SSD_Q), 1)
    for g in range(SSD_G):
        pcol = pcol_ref[g]
        prow = prow_ref[g]
        _ssd_one_direction(0, g, xf, bf, cf, dtcf, dtrf, pcol, prow, st_ref, yf_ref, row, col)
        _ssd_one_direction(1, g, xr, br, cr, dtcr, dtrr, pcol, prow, st_ref, yr_ref, row, col)


def _chunk_fwd(b, k):
    return jnp.where(k < NCH_C, NX // SSD_Q + b * NCH_C + k, b * NCH_X + (k - NCH_C))


def _chunk_rev(b, k):
    return jnp.where(k < NCH_C, NX // SSD_Q + b * NCH_C + (NCH_C - 1 - k),
                     b * NCH_X + (NCH_X - 1 - (k - NCH_C)))


def ssd_scan(xbc, p_ssd, dt_rows, pcol, prow):
    q = SSD_Q
    dtc0 = (SSD_INNER + SSD_CONV_CH) // 128

    dtcw = SSD_G * 128

    def specs(chunk):
        return [pl.BlockSpec((q, SSD_INNER), lambda b, k: (chunk(b, k), 0)),
                pl.BlockSpec((q, SSD_BC), lambda b, k: (chunk(b, k), SSD_INNER // SSD_BC)),
                pl.BlockSpec((q, SSD_BC), lambda b, k: (chunk(b, k), SSD_INNER // SSD_BC + 1)),
                pl.BlockSpec((q, dtcw), lambda b, k: (chunk(b, k), dtc0 * 128 // dtcw)),
                pl.BlockSpec((SSD_G * 8, q), lambda b, k: (0, chunk(b, k)))]

    def operands():
        return [xbc, xbc, xbc, p_ssd, dt_rows]

    out_spec_f = pl.BlockSpec((q, SSD_INNER), lambda b, k: (_chunk_fwd(b, k), 0))
    out_spec_r = pl.BlockSpec((q, SSD_INNER), lambda b, k: (_chunk_rev(b, k), 0))
    return pl.pallas_call(
        _ssd_kernel,
        grid=(NB, SSD_STEPS),
        in_specs=specs(_chunk_fwd) + specs(_chunk_rev) + [
            pl.BlockSpec((SSD_G, 8, 128), lambda b, k: (0, 0, 0)),
            pl.BlockSpec((SSD_G, 4, 8, 128), lambda b, k: (0, 0, 0, 0))],
        out_specs=[out_spec_f, out_spec_r],
        out_shape=[jax.ShapeDtypeStruct((NT, SSD_INNER), F32)] * 2,
        scratch_shapes=[pltpu.VMEM((2, SSD_G, SSD_HPG * SSD_P, SSD_N), F32)],
        compiler_params=_params(("parallel", "arbitrary")),
        name="ssd_scan",
    )(*operands(), *operands(), pcol, prow)


def _ssd_finish_kernel(yf_ref, yr_ref, xs_ref, z_ref, dsum_ref, g_ref, o_ref):
    z = z_ref[...]
    y = (yf_ref[...] + yr_ref[...] + dsum_ref[...] * xs_ref[...]) * (z * jax.nn.sigmoid(z))
    ms = jnp.mean(y * y, axis=-1, keepdims=True)
    o_ref[...] = (y * lax.rsqrt(ms + EPS) * g_ref[...]).astype(o_ref.dtype)


def ssd_finish(yf, yr, xbc, p_ssd, dsum, gain, rows):
    tm = 512
    blk = lambda: pl.BlockSpec((tm, SSD_INNER), lambda i: (i, 0))
    vec = lambda: pl.BlockSpec((1, SSD_INNER), lambda i: (0, 0))
    return pl.pallas_call(
        _ssd_finish_kernel,
        grid=(rows // tm,),
        in_specs=[blk(), blk(), blk(), blk(), vec(), vec()],
        out_specs=blk(),
        out_shape=jax.ShapeDtypeStruct((rows, SSD_INNER), BF16),
        compiler_params=_params(("parallel",)),
        name="ssd_finish",
    )(yf, yr, xbc, p_ssd, dsum, gain.reshape(1, SSD_INNER))


MLA_TM = 256
LKV = LX + LC


def _mla_prep_kernel(p_ref, wq_ref, wkv_ref, gq_ref, gkv_ref, tq1_ref, tq2_ref, tk1_ref, tk2_ref,
                     ct_ref, st_ref, q_ref, k_ref, v_ref):
    lane = lax.broadcasted_iota(jnp.int32, (MLA_TM, 128), 1)
    is_nope = lane < MLA_NOPE
    is_rope = jnp.logical_and(lane >= MLA_NOPE, lane < MLA_NOPE + MLA_ROPE)
    ct = ct_ref[...]
    st = st_ref[...]
    qa = p_ref[:, 0:MLA_QR]
    qn = (qa * lax.rsqrt(jnp.mean(qa * qa, axis=-1, keepdims=True) + EPS) * gq_ref[...]).astype(BF16)
    kva = p_ref[:, MLA_QR:MLA_QR + MLA_KVR]
    kvn = (kva * lax.rsqrt(jnp.mean(kva * kva, axis=-1, keepdims=True) + EPS) * gkv_ref[...]).astype(BF16)
    kp = p_ref[:, MLA_QR + MLA_KVR:]
    ss = jnp.sum(jnp.where(is_rope, kp * kp, 0.0), axis=-1, keepdims=True) * (1.0 / MLA_ROPE)
    r = lax.rsqrt(ss + EPS)
    kpe = kp * (r * ct * tk1_ref[1:2, :]) + pltpu.roll(kp, 96, 1) * (r * st * tk2_ref[...])
    rq_all = jnp.dot(qn, wq_ref[...], preferred_element_type=F32)
    rk_all = jnp.dot(kvn, wkv_ref[...], preferred_element_type=F32)
    tq1c = tq1_ref[...] * ct
    tq2s = tq2_ref[...] * st
    for h in range(MLA_H):
        rq = rq_all[:, h * 128:(h + 1) * 128]
        sq = rq * rq
        rn = lax.rsqrt(jnp.sum(jnp.where(is_nope, sq, 0.0), axis=-1, keepdims=True) * (1.0 / MLA_NOPE) + EPS)
        rp = lax.rsqrt(jnp.sum(jnp.where(is_rope, sq, 0.0), axis=-1, keepdims=True) * (1.0 / MLA_ROPE) + EPS)
        q_ref[0, h] = (rq * (jnp.where(is_nope, rn, rp) * tq1c)
                       + pltpu.roll(rq, 96, 1) * (rp * tq2s)).astype(BF16)
        rk = rk_all[:, h * 128:(h + 1) * 128]
        rkn = lax.rsqrt(jnp.sum(jnp.where(is_nope, rk * rk, 0.0), axis=-1, keepdims=True) * (1.0 / MLA_NOPE)
                        + EPS)
        k_ref[0, h] = jnp.where(is_nope, rk * rkn * tk1_ref[0:1, :], kpe).astype(BF16)
        v_ref[0, h] = rk.astype(BF16)


def _mla_tile_b(i):
    return jnp.where(i < NX // MLA_TM, i // (LX // MLA_TM), i - NX // MLA_TM)


def _mla_tile_r(i):
    return jnp.where(i < NX // MLA_TM, i % (LX // MLA_TM), LX // MLA_TM)


def mla_prep(p_mla, wq, wkv, gq, gkv, tq1, tq2, tk1, tk2, ctab, stab):
    ntile = NT // MLA_TM
    n_in = p_mla.shape[1]
    vec = lambda r: pl.BlockSpec((r, 128), lambda i: (0, 0))
    head_out = lambda: pl.BlockSpec((1, MLA_H, MLA_TM, 128), lambda i: (_mla_tile_b(i), 0, _mla_tile_r(i), 0))
    shp = jax.ShapeDtypeStruct((NB, MLA_H, LKV, 128), BF16)
    return pl.pallas_call(
        _mla_prep_kernel,
        grid=(ntile,),
        in_specs=[pl.BlockSpec((MLA_TM, n_in), lambda i: (i, 0)),
                  pl.BlockSpec((MLA_QR, MLA_H * 128), lambda i: (0, 0)),
                  pl.BlockSpec((MLA_KVR, MLA_H * 128), lambda i: (0, 0)),
                  pl.BlockSpec((1, MLA_QR), lambda i: (0, 0)),
                  pl.BlockSpec((1, MLA_KVR), lambda i: (0, 0)),
                  vec(1), vec(1), vec(2), vec(1),
                  pl.BlockSpec((MLA_TM, 128), lambda i: (i, 0)),
                  pl.BlockSpec((MLA_TM, 128), lambda i: (i, 0))],
        out_specs=[head_out(), head_out(), head_out()],
        out_shape=[shp, shp, shp],
        compiler_params=_params(("parallel",)),
        name="mla_prep",
    )(p_mla, wq, wkv, gq, gkv, tq1, tq2, tk1, tk2, ctab, stab)


ATT_TQ = 256
ATT_SCALE = (MLA_NOPE + MLA_ROPE) ** -0.5


def _attend(q, k, v):
    s = lax.dot_general(q, k, (((1,), (1,)), ((), ())), preferred_element_type=F32) * ATT_SCALE
    m = jnp.max(s, axis=-1, keepdims=True)
    e = jnp.exp(s - m)
    l = jnp.sum(e, axis=-1, keepdims=True)
    return jnp.dot(e.astype(BF16), v, preferred_element_type=F32) / l


def _attention_kernel(q_ref, k_ref, v_ref, o_ref, *, with_ctx):
    qi = pl.program_id(2)
    lane = lax.broadcasted_iota(jnp.int32, (ATT_TQ, 128), 1)

    def run(lo):
        o0 = _attend(q_ref[0, 0], k_ref[0, 0, lo:, :], v_ref[0, 0, lo:, :])
        o1 = _attend(q_ref[0, 1], k_ref[0, 1, lo:, :], v_ref[0, 1, lo:, :])
        o_ref[...] = jnp.where(lane < MLA_V, pltpu.roll(o0, 64, 1), o1).astype(o_ref.dtype)

    if with_ctx:
        @pl.when(qi < LX // ATT_TQ)
        def _():
            run(0)

        @pl.when(qi == LX // ATT_TQ)
        def _():
            run(LX)
    else:
        run(0)


def attention(qh, kh, vh, with_ctx):
    nq = LX // ATT_TQ + (1 if with_ctx else 0)
    rows = NT if with_ctx else NX

    def out_row(b, qi):
        return jnp.where(qi < LX // ATT_TQ, b * (LX // ATT_TQ) + qi, NX // ATT_TQ + b)

    kv_spec = lambda: pl.BlockSpec((1, 2, LKV, 128), lambda b, p, qi: (b, p, 0, 0))
    return pl.pallas_call(
        functools.partial(_attention_kernel, with_ctx=with_ctx),
        grid=(NB, MLA_H // 2, nq),
        in_specs=[pl.BlockSpec((1, 2, ATT_TQ, 128), lambda b, p, qi: (b, p, qi, 0)), kv_spec(), kv_spec()],
        out_specs=pl.BlockSpec((ATT_TQ, 128), lambda b, p, qi: (out_row(b, qi), p)),
        out_shape=jax.ShapeDtypeStruct((rows, MLA_H * MLA_V), BF16),
        compiler_params=_params(("parallel", "parallel", "arbitrary")),
        name="mla_attention",
    )(qh, kh, vh)


S5_RX = NX // S5_T
S5_RC = NC // S5_T
S5_R = S5_RX + S5_RC
S5_W = S5_T * 128
S5_ST = 2 * S5_GB * S5_P


def _s5_scan_rows(s_scr, xin_scr, a_re, a_im, reverse):
    half = S5_ST // 2
    cx = LX // S5_T
    cc = LC // S5_T
    zero = jnp.zeros((1, half), F32)

    def sweep(base_of, n, state):
        def body(i, st):
            c = (n - 1 - i) if reverse else i
            new = []
            for b in range(NB):
                r = base_of(b) + c
                xr, xi = st[b]
                xin_scr[pl.ds(r, 1), :] = jnp.concatenate([xr, xi], axis=1)
                srow = s_scr[pl.ds(r, 1), :]
                new.append((a_re * xr - a_im * xi + srow[:, :half],
                            a_re * xi + a_im * xr + srow[:, half:]))
            return tuple(new)
        return lax.fori_loop(0, n, body, state)

    state = tuple((zero, zero) for _ in range(NB))
    state = sweep(lambda b: S5_RX + b * cc, cc, state)
    sweep(lambda b: b * cx, cx, state)


def _s5_kernel(u_ref, tiles_ref, win_ref, wout_ref, at_ref, y_ref, m_scr, s_scr, xin_scr):
    d = pl.program_id(1)
    ucat = jnp.concatenate([u_ref[pl.ds(t, S5_R, stride=S5_T), :] for t in range(S5_T)],
                           axis=1).astype(BF16)
    s_scr[...] = jnp.dot(ucat, win_ref[0, 0], preferred_element_type=F32)
    a_re = at_ref[0, 0][:, :S5_ST // 2]
    a_im = at_ref[0, 0][:, S5_ST // 2:]
    zero_tile = jnp.zeros((128, 128), BF16)

    def build(reverse):
        for tp in range(S5_T):
            for t in range(S5_T):
                lag = (tp - t) if reverse else (t - tp)
                m_scr[tp * 128:(tp + 1) * 128, t * 128:(t + 1) * 128] = (
                    tiles_ref[0, 0, lag] if lag >= 0 else zero_tile)

    @pl.when(d == 0)
    def _():
        build(False)
        _s5_scan_rows(s_scr, xin_scr, a_re, a_im, False)

    @pl.when(d == 1)
    def _():
        build(True)
        _s5_scan_rows(s_scr, xin_scr, a_re, a_im, True)

    ycat = (jnp.dot(ucat, m_scr[...], preferred_element_type=F32)
            + jnp.dot(xin_scr[...].astype(BF16), wout_ref[0, 0], preferred_element_type=F32))

    @pl.when(d == 0)
    def _():
        for t in range(S5_T):
            y_ref[pl.ds(t, S5_R, stride=S5_T), :] = ycat[:, t * 128:(t + 1) * 128]

    @pl.when(d == 1)
    def _():
        for t in range(S5_T):
            y_ref[pl.ds(t, S5_R, stride=S5_T), :] += ycat[:, t * 128:(t + 1) * 128]


def s5_conv(u, tiles, win, wout, at):
    nblk = S5_INNER // 128
    return pl.pallas_call(
        _s5_kernel,
        grid=(nblk, 2),
        in_specs=[pl.BlockSpec((NT, 128), lambda g, d: (0, g)),
                  pl.BlockSpec((1, 1, S5_T, 128, 128), lambda g, d: (d, g, 0, 0, 0)),
                  pl.BlockSpec((1, 1, S5_W, S5_ST), lambda g, d: (d, g, 0, 0)),
                  pl.BlockSpec((1, 1, S5_ST, S5_W), lambda g, d: (d, g, 0, 0)),
                  pl.BlockSpec((1, 1, 1, S5_ST), lambda g, d: (d, g, 0, 0))],
        out_specs=pl.BlockSpec((NT, 128), lambda g, d: (0, g)),
        out_shape=jax.ShapeDtypeStruct((NT, S5_INNER), F32),
        scratch_shapes=[pltpu.VMEM((S5_W, S5_W), BF16), pltpu.VMEM((S5_R, S5_ST), F32),
                        pltpu.VMEM((S5_R, S5_ST), F32)],
        compiler_params=_params(("parallel", "arbitrary")),
        name="s5_conv",
    )(u, tiles, win, wout, at)


def s5_operators(lam_re, lam_im, log_dt, b_re, b_im, c_re, c_im):
    t_len = S5_T
    dt = jnp.exp(log_dt)[..., None]
    mag = jnp.exp(lam_re * dt)
    ab_re = mag * jnp.cos(lam_im * dt)
    ab_im = mag * jnp.sin(lam_im * dt)
    den = lam_re * lam_re + lam_im * lam_im
    nr = ab_re - 1.0
    f_re = ((nr * lam_re + ab_im * lam_im) / den)[..., None]
    f_im = ((ab_im * lam_re - nr * lam_im) / den)[..., None]
    bb_re = f_re * b_re - f_im * b_im
    bb_im = f_re * b_im + f_im * b_re
    pr, pi = [jnp.ones_like(ab_re)], [jnp.zeros_like(ab_re)]
    for _ in range(t_len):
        pr, pi = pr + [pr[-1] * ab_re - pi[-1] * ab_im], pi + [pr[-1] * ab_im + pi[-1] * ab_re]
    pw_re = jnp.stack(pr)
    pw_im = jnp.stack(pi)
    cp_re = c_re[None] * pw_re[:, :, :, None, :] - c_im[None] * pw_im[:, :, :, None, :]
    cp_im = c_re[None] * pw_im[:, :, :, None, :] + c_im[None] * pw_re[:, :, :, None, :]
    kj = (jnp.einsum("jdgip,dgps->jdgis", cp_re[:t_len], bb_re, precision=HIGHEST)
          - jnp.einsum("jdgip,dgps->jdgis", cp_im[:t_len], bb_im, precision=HIGHEST))
    nb = S5_GROUPS // S5_GB
    gl = jnp.arange(S5_GB)
    kj = kj.reshape(t_len, 2, nb, S5_GB, S5_S, S5_S).transpose(1, 2, 0, 3, 5, 4)
    diag_t = (gl[:, None, None, None] == gl[None, None, :, None])
    tiles = jnp.where(diag_t, kj[:, :, :, :, :, None, :], 0.0).reshape(2, nb, t_len, 128, 128)
    ab_r = pw_re[:t_len, :, :, :, None] * bb_re[None] - pw_im[:t_len, :, :, :, None] * bb_im[None]
    ab_i = pw_re[:t_len, :, :, :, None] * bb_im[None] + pw_im[:t_len, :, :, :, None] * bb_re[None]
    ab = jnp.stack([ab_r, ab_i], axis=-1)
    ab = jnp.stack([ab[::-1, 0], ab[:, 1]], axis=1)
    ab = ab.reshape(t_len, 2, nb, S5_GB, S5_P, S5_S, 2).transpose(1, 2, 0, 3, 5, 6, 4)
    diag_w = (gl[:, None, None, None, None] == gl[None, None, None, :, None])
    win = jnp.where(diag_w, ab[:, :, :, :, :, :, None, :], 0.0).reshape(2, nb, t_len * 128, S5_ST)
    q_re = jnp.stack([cp_re[1:, 0], cp_re[:0:-1, 1]], axis=1)
    q_im = jnp.stack([cp_im[1:, 0], cp_im[:0:-1, 1]], axis=1)
    qq = jnp.stack([q_re, -q_im], axis=-1)
    qq = qq.reshape(t_len, 2, nb, S5_GB, S5_S, S5_P, 2).transpose(1, 2, 6, 3, 5, 0, 4)
    diag_o = (gl[:, None, None, None, None] == gl[None, None, None, :, None])
    wout = jnp.where(diag_o, qq[:, :, :, :, :, :, None, :], 0.0).reshape(2, nb, S5_ST, t_len * 128)
    at = jnp.stack([pw_re[t_len], pw_im[t_len]], axis=1)
    at = at.reshape(2, 2, nb, S5_GB * S5_P).transpose(0, 2, 1, 3).reshape(2, nb, 1, S5_ST)
    return tiles.astype(BF16), win.astype(BF16), wout.astype(BF16), at


def _s5_finish_kernel(y_ref, u_ref, d_ref, w_ref, o_ref):
    y = y_ref[...] + d_ref[...] * u_ref[...]
    v = jax.nn.gelu(y)
    gate = jnp.dot(v.astype(BF16), w_ref[...], preferred_element_type=F32)
    o_ref[...] = (v * jax.nn.sigmoid(gate)).astype(o_ref.dtype)


def s5_finish(y, u, d_skip, w_glu, rows):
    tm = 512
    blk = lambda: pl.BlockSpec((tm, S5_INNER), lambda i: (i, 0))
    return pl.pallas_call(
        _s5_finish_kernel,
        grid=(rows // tm,),
        in_specs=[blk(), blk(), pl.BlockSpec((1, S5_INNER), lambda i: (0, 0)),
                  pl.BlockSpec((S5_INNER, S5_INNER), lambda i: (0, 0))],
        out_specs=blk(),
        out_shape=jax.ShapeDtypeStruct((rows, S5_INNER), BF16),
        compiler_params=_params(("parallel",)),
        name="s5_finish",
    )(y, u, d_skip.reshape(1, S5_INNER), w_glu)


def _merge_kernel(ssd_ref, mla_ref, s5_ref, g1_ref, g2_ref, g3_ref, w1_ref, w2_ref, w3_ref, o_ref):
    m = (g1_ref[...] * jnp.dot(ssd_ref[...], w1_ref[...], preferred_element_type=F32)
         + g2_ref[...] * jnp.dot(mla_ref[...], w2_ref[...], preferred_element_type=F32)
         + g3_ref[...] * jnp.dot(s5_ref[...], w3_ref[...], preferred_element_type=F32))
    o_ref[...] = m.astype(o_ref.dtype)


def merge(ssd_o, mla_o, s5_o, gates, w1, w2, w3, rows):
    tm, tn = 512, 512
    nj = D // tn
    act = lambda: pl.BlockSpec((tm, 1024), lambda i, j: (i, 0))
    wsp = lambda: pl.BlockSpec((1024, tn), lambda i, j: (0, j))
    gsp = lambda k: pl.BlockSpec((tm, tn), lambda i, j: (i, k * nj + j))
    return pl.pallas_call(
        _merge_kernel,
        grid=(rows // tm, nj),
        in_specs=[act(), act(), act(), gsp(0), gsp(1), gsp(2), wsp(), wsp(), wsp()],
        out_specs=pl.BlockSpec((tm, tn), lambda i, j: (i, j)),
        out_shape=jax.ShapeDtypeStruct((rows, D), BF16),
        compiler_params=_params(("parallel", "parallel")),
        name="merge",
    )(ssd_o, mla_o, s5_o, gates, gates, gates, w1, w2, w3)


def _outproj_kernel(m_ref, w_ref, x_ref, g_ref, o_ref):
    o_ref[...] = x_ref[...] + g_ref[0] * jnp.dot(m_ref[...], w_ref[...], preferred_element_type=F32)


def outproj(m, w_out, x, modtab, k_gate, rows):
    tm, tn = 512, 512
    tpb = LX // tm
    return pl.pallas_call(
        _outproj_kernel,
        grid=(rows // tm, D // tn),
        in_specs=[pl.BlockSpec((tm, D), lambda i, j: (i, 0)),
                  pl.BlockSpec((D, tn), lambda i, j: (0, j)),
                  pl.BlockSpec((tm, tn), lambda i, j: (i, j)),
                  pl.BlockSpec((1, 1, tn), lambda i, j: (_mod_row(i, tpb) * 6 + k_gate, 0, j))],
        out_specs=pl.BlockSpec((tm, tn), lambda i, j: (i, j)),
        out_shape=jax.ShapeDtypeStruct((rows, D), F32),
        compiler_params=_params(("parallel", "parallel")),
        name="outproj",
    )(m, w_out, x, modtab)


def _router_kernel(x_ref, g_ref, sc_ref, sh_ref, w_ref, h_ref, o_ref):
    x = x_ref[...]
    ms = jnp.mean(x * x, axis=-1, keepdims=True)
    h = x * lax.rsqrt(ms + EPS) * g_ref[...] * (1.0 + sc_ref[0]) + sh_ref[0]
    h_ref[...] = h.astype(BF16)
    logits = lax.dot_general(w_ref[...], h, (((1,), (1,)), ((), ())),
                             precision=HIGHEST, preferred_element_type=F32)
    m = jnp.max(logits, axis=0, keepdims=True)
    e = jnp.exp(logits - m)
    o_ref[...] = e / jnp.sum(e, axis=0, keepdims=True)


def router(x, gain, modtab, w_router_t, rows):
    tm = 256
    tpb = LX // tm
    return pl.pallas_call(
        _router_kernel,
        grid=(rows // tm,),
        in_specs=[pl.BlockSpec((tm, D), lambda i: (i, 0)),
                  pl.BlockSpec((1, D), lambda i: (0, 0)),
                  pl.BlockSpec((1, 1, D), lambda i: (_mod_row(i, tpb) * 6 + 4, 0, 0)),
                  pl.BlockSpec((1, 1, D), lambda i: (_mod_row(i, tpb) * 6 + 3, 0, 0)),
                  pl.BlockSpec((N_EXP, D), lambda i: (0, 0))],
        out_specs=[pl.BlockSpec((tm, D), lambda i: (i, 0)),
                   pl.BlockSpec((N_EXP, tm), lambda i: (0, i))],
        out_shape=[jax.ShapeDtypeStruct((rows, D), BF16), jax.ShapeDtypeStruct((N_EXP, rows), F32)],
        compiler_params=_params(("parallel",)),
        name="moe_router",
    )(x, gain.reshape(1, D), modtab, modtab, w_router_t)


def _prefix_count(mask_f, n):
    row = lax.broadcasted_iota(jnp.int32, (128, 128), 0)
    col = lax.broadcasted_iota(jnp.int32, (128, 128), 1)
    upper = (row < col).astype(BF16)
    offset = jnp.zeros((N_EXP, 1), F32)
    parts = []
    for j in range(n // 128):
        blk = mask_f[:, j * 128:(j + 1) * 128]
        parts.append(jnp.dot(blk.astype(BF16), upper, preferred_element_type=F32) + offset)
        offset = offset + jnp.sum(blk, axis=1, keepdims=True)
    return jnp.concatenate(parts, axis=1)


def _select_kernel(aff_ref, g_ref, pos_ref, *, n, cap):
    aff = aff_ref[...]
    capf = float(cap)

    def count_ge(t):
        return jnp.sum((aff >= t).astype(F32), axis=1, keepdims=True)

    def bisect(_, c):
        lo, hi = c
        mid = 0.5 * (lo + hi)
        ok = count_ge(mid) >= capf
        return jnp.where(ok, mid, lo), jnp.where(ok, hi, mid)

    lo, hi = lax.fori_loop(0, 40, bisect, (jnp.zeros((N_EXP, 1), F32), jnp.full((N_EXP, 1), 2.0, F32)))

    def unfinished(c):
        return jnp.min(c[2]) < 0.5

    def step(c):
        cur, thr, done = c
        cand = jnp.max(jnp.where(aff < cur, aff, -1.0), axis=1, keepdims=True)
        ok = (count_ge(cand) >= capf).astype(F32)
        thr = jnp.where(done > 0.5, thr, cand)
        done = jnp.maximum(done, ok)
        return jnp.where(done > 0.5, cur, cand), thr, done

    _, thr, _ = lax.while_loop(unfinished, step, (hi, lo, jnp.zeros((N_EXP, 1), F32)))
    gt = aff > thr
    eq = (aff == thr).astype(F32)
    need = cap - jnp.sum(gt.astype(F32), axis=1, keepdims=True)
    sel = jnp.logical_or(gt, jnp.logical_and(eq > 0, _prefix_count(eq, n) < need))
    sel_f = sel.astype(F32)
    g_ref[...] = jnp.where(sel, aff, 0.0)
    pos_ref[...] = jnp.where(sel, _prefix_count(sel_f, n), -1.0)


def select(aff_t, n, cap, row0):
    b0 = row0 // n
    spec = lambda: pl.BlockSpec((N_EXP, n), lambda b: (0, b))
    shp = jax.ShapeDtypeStruct((N_EXP, NB * n), F32)
    return pl.pallas_call(
        functools.partial(_select_kernel, n=n, cap=cap),
        grid=(NB,),
        in_specs=[pl.BlockSpec((N_EXP, n), lambda b: (0, b0 + b))],
        out_specs=[spec(), spec()],
        out_shape=[shp, shp],
        compiler_params=_params(("parallel",)),
        name="moe_select",
    )(aff_t)


def _gather_kernel(h_ref, pos_ref, g_ref, xs_ref, gs_ref, *, n, cap):
    pos = pos_ref[0, 0]
    slot = lax.broadcasted_iota(jnp.int32, (cap, n), 0).astype(F32)
    onehot = slot == pos
    xs_ref[0] = jnp.dot(onehot.astype(BF16), h_ref[...], preferred_element_type=F32).astype(BF16)
    gsel = jnp.sum(jnp.where(onehot, g_ref[0, 0], 0.0), axis=1, keepdims=True)
    gs_ref[0] = jnp.broadcast_to(gsel, (cap, 128))


def gather(h2, pos_t, g_t, n, cap, row0):
    b0 = row0 // n
    tab = lambda: pl.BlockSpec((1, 1, 1, n), lambda b, e: (e, b, 0, 0))
    return pl.pallas_call(
        functools.partial(_gather_kernel, n=n, cap=cap),
        grid=(NB, N_EXP),
        in_specs=[pl.BlockSpec((n, D), lambda b, e: (b0 + b, 0)), tab(), tab()],
        out_specs=[pl.BlockSpec((1, cap, D), lambda b, e: (e, b, 0)),
                   pl.BlockSpec((1, cap, 128), lambda b, e: (e, b, 0))],
        out_shape=[jax.ShapeDtypeStruct((N_EXP, NB * cap, D), BF16),
                   jax.ShapeDtypeStruct((N_EXP, NB * cap, 128), F32)],
        compiler_params=_params(("parallel", "arbitrary")),
        name="moe_gather",
    )(h2, pos_t, g_t)


EXP_TF = 256


def _expert_kernel(*refs, with_ctx):
    if with_ctx:
        xs_ref, xc_ref, gs_ref, gc_ref, wg_ref, wu_ref, wd_ref, ys_ref, yc_ref, acc_ref = refs
        xs = jnp.concatenate([xs_ref[0], xc_ref[0]], axis=0)
    else:
        xs_ref, gs_ref, wg_ref, wu_ref, wd_ref, ys_ref, acc_ref = refs
        xs = xs_ref[0]
    f = pl.program_id(1)
    hg = jnp.dot(xs, wg_ref[0].astype(BF16), preferred_element_type=F32)
    hu = jnp.dot(xs, wu_ref[0].astype(BF16), preferred_element_type=F32)
    hid = (hg * jax.nn.sigmoid(hg) * hu).astype(BF16)
    contrib = jnp.dot(hid, wd_ref[0].astype(BF16), preferred_element_type=F32)

    @pl.when(f == 0)
    def _():
        acc_ref[...] = contrib

    @pl.when(f > 0)
    def _():
        acc_ref[...] += contrib

    @pl.when(f == FF // EXP_TF - 1)
    def _():
        rx = NB * CAP_X
        ys_ref[0] = (acc_ref[0:rx, :] * gs_ref[0][:, 0:1]).astype(BF16)
        if with_ctx:
            yc_ref[0] = (acc_ref[rx:, :] * gc_ref[0][:, 0:1]).astype(BF16)


def experts(xs, gs, w_gate, w_up, w_down, layer, xc=None, gc=None):
    with_ctx = xc is not None
    rx, rc = NB * CAP_X, NB * CAP_C
    e0 = layer * N_EXP
    tok = lambda r, w: pl.BlockSpec((1, r, w), lambda e, f: (e, 0, 0))
    in_specs = [tok(rx, D)] + ([tok(rc, D)] if with_ctx else []) + [tok(rx, 128)] + ([tok(rc, 128)] if with_ctx else [])
    in_specs += [pl.BlockSpec((1, D, EXP_TF), lambda e, f: (e0 + e, 0, f)),
                 pl.BlockSpec((1, D, EXP_TF), lambda e, f: (e0 + e, 0, f)),
                 pl.BlockSpec((1, EXP_TF, D), lambda e, f: (e0 + e, f, 0))]
    out_specs = [tok(rx, D)] + ([tok(rc, D)] if with_ctx else [])
    out_shape = [jax.ShapeDtypeStruct((N_EXP, rx, D), BF16)] + (
        [jax.ShapeDtypeStruct((N_EXP, rc, D), BF16)] if with_ctx else [])
    args = [xs] + ([xc] if with_ctx else []) + [gs] + ([gc] if with_ctx else []) + [w_gate, w_up, w_down]
    return pl.pallas_call(
        functools.partial(_expert_kernel, with_ctx=with_ctx),
        grid=(N_EXP, FF // EXP_TF),
        in_specs=in_specs,
        out_specs=out_specs,
        out_shape=out_shape,
        scratch_shapes=[pltpu.VMEM((rx + (rc if with_ctx else 0), D), F32)],
        compiler_params=_params(("parallel", "arbitrary")),
        name="moe_experts",
    )(*args)


def _combine_kernel(*refs, cap, aliased):
    pos_ref, ys_ref, x_ref, g_ref, o_ref = refs[1:] if aliased else refs
    pos = pos_ref[...]
    tm = pos.shape[0]
    cap_pad = max(cap, 128)
    slot = lax.broadcasted_iota(jnp.int32, (tm, cap_pad), 1).astype(F32)
    acc = None
    for e in range(N_EXP):
        onehot = (pos[:, e:e + 1] == slot).astype(BF16)
        ys = ys_ref[e]
        if cap_pad > cap:
            ys = jnp.concatenate([ys, jnp.zeros((cap_pad - cap, D), BF16)], axis=0)
        part = jnp.dot(onehot, ys, preferred_element_type=F32)
        acc = part if acc is None else acc + part
    o_ref[...] = x_ref[...] + g_ref[0] * acc


def combine(pos_col, ys, x, modtab, k_gate, n, cap, row0, mod_row_of_batch, out_rows, prev=None):
    tm = 256
    tpb = n // tm
    r0 = row0 // tm
    aliased = prev is not None
    in_specs = [pl.BlockSpec((tm, N_EXP), lambda b, t: (b * tpb + t, 0)),
                pl.BlockSpec((N_EXP, cap, D), lambda b, t: (0, b, 0)),
                pl.BlockSpec((tm, D), lambda b, t: (r0 + b * tpb + t, 0)),
                pl.BlockSpec((1, 1, D), lambda b, t: (mod_row_of_batch(b) * 6 + k_gate, 0, 0))]
    args = [pos_col, ys, x, modtab]
    if aliased:
        in_specs = [pl.BlockSpec(memory_space=pl.ANY)] + in_specs
        args = [prev] + args
    return pl.pallas_call(
        functools.partial(_combine_kernel, cap=cap, aliased=aliased),
        grid=(NB, tpb),
        in_specs=in_specs,
        out_specs=pl.BlockSpec((tm, D), lambda b, t: (r0 + b * tpb + t, 0)),
        out_shape=jax.ShapeDtypeStruct((out_rows, D), F32),
        input_output_aliases={0: 0} if aliased else {},
        compiler_params=_params(("parallel", "parallel")),
        name="moe_combine",
    )(*args)


def moe_layer(xa, gain, modtab, w_router_t, w_gate, w_up, w_down, layer, with_ctx):
    rows = NT if with_ctx else NX
    h2, aff = router(xa, gain, modtab, w_router_t, rows)

    def route(n, cap, row0):
        g_t, pos_t = select(aff, n, cap, row0)
        xs, gs = gather(h2, pos_t.reshape(N_EXP, NB, 1, n), g_t.reshape(N_EXP, NB, 1, n), n, cap, row0)
        return pos_t.T, xs, gs

    pos_x, xs_x, gs_x = route(LX, CAP_X, 0)
    if with_ctx:
        pos_c, xs_c, gs_c = route(LC, CAP_C, NX)
        ys_x, ys_c = experts(xs_x, gs_x, w_gate, w_up, w_down, layer, xs_c, gs_c)
    else:
        (ys_x,) = experts(xs_x, gs_x, w_gate, w_up, w_down, layer)
    out = combine(pos_x, ys_x, xa, modtab, 5, LX, CAP_X, 0, lambda b: b, rows)
    if with_ctx:
        out = combine(pos_c, ys_c, xa, modtab, 5, LC, CAP_C, NX, lambda b: NB, rows, prev=out)
    return out


def _rope_tables():
    n_freq = MLA_ROPE // 4
    inv = ROPE_BASE ** (-jnp.arange(n_freq, dtype=F32) / n_freq)
    rows = jnp.repeat(jnp.arange(LX // GRID_W, dtype=F32), GRID_W)
    cols = jnp.tile(jnp.arange(GRID_W, dtype=F32), LX // GRID_W)
    ang = jnp.concatenate([rows[:, None] * inv, cols[:, None] * inv], axis=-1)
    cos, sin = jnp.cos(ang), jnp.sin(ang)
    one = jnp.ones((LX, MLA_NOPE), F32)
    zero = jnp.zeros((LX, 32), F32)
    ct_x = jnp.concatenate([one, cos, cos, zero], axis=1)
    st_x = jnp.concatenate([0 * one, -sin, sin, zero], axis=1)
    ct_c = jnp.concatenate([jnp.ones((NC, 96), F32), jnp.zeros((NC, 32), F32)], axis=1)
    ctab = jnp.concatenate([jnp.tile(ct_x, (NB, 1)), ct_c], axis=0)
    stab = jnp.concatenate([jnp.tile(st_x, (NB, 1)), jnp.zeros((NC, 128), F32)], axis=0)
    return ctab, stab


_PERM = list(range(0, MLA_ROPE, 2)) + list(range(1, MLA_ROPE, 2))
_PERM_SW = list(range(1, MLA_ROPE, 2)) + list(range(0, MLA_ROPE, 2))


def _lane_table(first64, rope32):
    return jnp.concatenate([first64, rope32, jnp.zeros((32,), F32)]).reshape(1, 128)


def _layer_weights(i, w_in, ssd_a_log, ssd_dt_bias, mla_w_q_b, mla_w_kv_b, mla_q_gain, mla_k_gain):
    w = w_in[i]
    wdt = w[:, SSD_INNER + SSD_CONV_CH:O1].reshape(D, SSD_G, SSD_HPG)
    wdt_col = jnp.pad(wdt, ((0, 0), (0, 0), (0, 128 - SSD_HPG))).reshape(D, SSD_G * 128)
    w_ssd = jnp.concatenate([w[:, :SSD_INNER + SSD_CONV_CH], wdt_col], axis=1).astype(BF16)
    wdt_row = jnp.pad(jnp.transpose(wdt, (1, 2, 0)), ((0, 0), (0, 8 - SSD_HPG), (0, 0))).reshape(SSD_G * 8, D)
    kpe = w[:, O1 + MLA_QR + MLA_KVR:O2]
    kpe128 = jnp.concatenate([jnp.zeros((D, MLA_NOPE), F32), kpe[:, _PERM], kpe[:, _PERM_SW]], axis=1)
    w_mla = jnp.concatenate([w[:, O1:O1 + MLA_QR + MLA_KVR], kpe128], axis=1).astype(BF16)
    w_s5 = w[:, O2:O3].astype(BF16)
    w_gate = w[:, O3:].astype(BF16)
    par = jnp.stack([ssd_a_log[i, 0], ssd_a_log[i, 1], ssd_dt_bias[i, 0], ssd_dt_bias[i, 1]])
    par = par.reshape(4, SSD_G, SSD_HPG).transpose(1, 0, 2)
    pcol = jnp.pad(par, ((0, 0), (0, 4), (0, 128 - SSD_HPG)))
    prow = jnp.broadcast_to(jnp.pad(par, ((0, 0), (0, 0), (0, 8 - SSD_HPG)))[..., None], (SSD_G, 4, 8, 128))
    wq = mla_w_q_b[i].reshape(MLA_QR, MLA_H, MLA_NOPE + MLA_ROPE)
    wq = jnp.concatenate([wq[..., :MLA_NOPE], wq[..., MLA_NOPE:][..., _PERM], wq[..., MLA_NOPE:][..., _PERM_SW]],
                         axis=-1).reshape(MLA_QR, MLA_H * 128).astype(BF16)
    wkv = mla_w_kv_b[i].astype(BF16)
    qg, kg = mla_q_gain[i], mla_k_gain[i]
    tq1 = _lane_table(qg[:MLA_NOPE], qg[MLA_NOPE:][jnp.array(_PERM)])
    tq2 = _lane_table(jnp.zeros((MLA_NOPE,), F32), qg[MLA_NOPE:][jnp.array(_PERM_SW)])
    tk1 = jnp.concatenate([_lane_table(kg[:MLA_NOPE], jnp.zeros((MLA_ROPE,), F32)),
                           _lane_table(jnp.zeros((MLA_NOPE,), F32), kg[MLA_NOPE:][jnp.array(_PERM)])], axis=0)
    tk2 = _lane_table(jnp.zeros((MLA_NOPE,), F32), kg[MLA_NOPE:][jnp.array(_PERM_SW)])
    return dict(w_ssd=w_ssd, wdt_row=wdt_row.astype(BF16), w_mla=w_mla, w_s5=w_s5, w_gate=w_gate,
                pcol=pcol, prow=prow, wq=wq, wkv=wkv, tq1=tq1, tq2=tq2, tk1=tk1, tk2=tk2)


def kernel(x, c, ctx, c_ctx, norm1_gain, norm2_gain, w_mod, b_mod, w_in, ssd_conv_w, ssd_conv_b, ssd_a_log, ssd_dt_bias, ssd_d, ssd_norm_gain, mla_q_a_gain, mla_kv_a_gain, mla_w_q_b, mla_w_kv_b, mla_q_gain, mla_k_gain, s5_lam_re, s5_lam_im, s5_log_dt, s5_b_re, s5_b_im, s5_c_re, s5_c_im, s5_d, s5_w_glu, w_branch_ssd, w_branch_mla, w_branch_s5, w_out, moe_router, moe_w_gate, moe_w_up, moe_w_down):
    ctab, stab = _rope_tables()
    cvec = jnp.concatenate([c, c_ctx[None], jnp.zeros((8 - NB - 1, D), F32)], axis=0)
    mod = modulation(cvec, w_mod, b_mod)
    xa = jnp.concatenate([x.reshape(NX, D), ctx.reshape(NC, D)], axis=0)
    for i in range(DEPTH):
        need_ctx = i < DEPTH - 1
        rows = NT if need_ctx else NX
        modtab = mod[i].reshape(8 * 6, 1, D)
        lw = _layer_weights(i, w_in, ssd_a_log, ssd_dt_bias, mla_w_q_b, mla_w_kv_b, mla_q_gain, mla_k_gain)
        h = normmod(xa, norm1_gain[i], modtab, 0, 1, NT)
        p_ssd = matmul(h, lw["w_ssd"], 1024, 512, name="inproj_ssd")
        dt_rows = matmul_nt(lw["wdt_row"], h, 1024)
        p_mla = matmul(h, lw["w_mla"], 1024, 896, name="inproj_mla")
        u_s5 = matmul(h, lw["w_s5"], 1024, 512, name="inproj_s5")
        gates = matmul(h, lw["w_gate"], 1024, 1024, act="sigmoid", name="inproj_gate", rows=rows)
        xbc = ssd_conv(p_ssd, ssd_conv_w[i], ssd_conv_b[i])
        yf, yr = ssd_scan(xbc, p_ssd, dt_rows, lw["pcol"], lw["prow"])
        dsum = jnp.repeat(ssd_d[i, 0] + ssd_d[i, 1], SSD_P).reshape(1, SSD_INNER)
        ssd_o = ssd_finish(yf, yr, xbc, p_ssd, dsum, ssd_norm_gain[i], rows)
        qh, kh, vh = mla_prep(p_mla, lw["wq"], lw["wkv"], mla_q_a_gain[i].reshape(1, MLA_QR),
                              mla_kv_a_gain[i].reshape(1, MLA_KVR), lw["tq1"], lw["tq2"], lw["tk1"], lw["tk2"],
                              ctab, stab)
        mla_o = attention(qh, kh, vh, need_ctx)
        ops = s5_operators(s5_lam_re[i], s5_lam_im[i], s5_log_dt[i], s5_b_re[i], s5_b_im[i], s5_c_re[i], s5_c_im[i])
        y_s5 = s5_conv(u_s5, *ops)
        s5_o = s5_finish(y_s5, u_s5, s5_d[i], s5_w_glu[i].astype(BF16), rows)
        m = merge(ssd_o, mla_o, s5_o, gates, w_branch_ssd[i].astype(BF16), w_branch_mla[i].astype(BF16),
                  w_branch_s5[i].astype(BF16), rows)
        xa = outproj(m, w_out[i].astype(BF16), xa, modtab, 2, rows)
        xa = moe_layer(xa, norm2_gain[i], modtab, moe_router[i].T, moe_w_gate.reshape(DEPTH * N_EXP, D, FF),
                       moe_w_up.reshape(DEPTH * N_EXP, D, FF), moe_w_down.reshape(DEPTH * N_EXP, FF, D), i, need_ctx)
    return xa[:NX].reshape(NB, LX, D)
```

```python
import functools
import math

import jax
import jax.numpy as jnp
from jax import lax
from jax.experimental import pallas as pl
from jax.experimental.pallas import tpu as pltpu

F32 = jnp.float32
BF16 = jnp.bfloat16
HIGHEST = lax.Precision.HIGHEST

D = 2048
NB = 4
LX = 2048
LC = 256
NX = NB * LX
NC = NB * LC
NT = NX + NC
DEPTH = 2
EPS = 1e-6
GRID_W = 64

SSD_HEADS = 16
SSD_P = 64
SSD_INNER = 1024
SSD_G = 4
SSD_HPG = 4
SSD_N = 128
SSD_Q = 128
SSD_BC = SSD_G * SSD_N
SSD_CONV_CH = SSD_INNER + 2 * SSD_BC
SSD_IN = SSD_INNER + SSD_CONV_CH + SSD_HEADS

MLA_H = 16
MLA_QR = 512
MLA_KVR = 256
MLA_NOPE = 64
MLA_ROPE = 32
MLA_V = 64
MLA_IN = MLA_QR + MLA_KVR + MLA_ROPE
ROPE_BASE = 10000.0

S5_INNER = 1024
S5_S = 16
S5_GROUPS = 64
S5_P = 64
S5_T = 8
S5_GB = 8

N_EXP = 16
FF = 2048
CAP_X = 2 * LX // N_EXP
CAP_C = 2 * LC // N_EXP

O1 = SSD_IN
O2 = O1 + MLA_IN
O3 = O2 + S5_INNER

VMEM_LIMIT = 56 * 1024 * 1024


def _params(sem):
    return pltpu.CompilerParams(dimension_semantics=sem, vmem_limit_bytes=VMEM_LIMIT)


def _mod_row(i, tiles_per_batch):
    return jnp.minimum(i // tiles_per_batch, NB)


def _modulation_kernel(c_ref, w_ref, b_ref, o_ref):
    c = c_ref[...]
    a = (c * jax.nn.sigmoid(c)).astype(BF16)
    o_ref[0] = jnp.dot(a, w_ref[0].astype(BF16), preferred_element_type=F32) + b_ref[0]


def modulation(cvec, w_mod, b_mod):
    tn = 1536
    n = w_mod.shape[-1]
    return pl.pallas_call(
        _modulation_kernel,
        grid=(DEPTH, n // tn),
        in_specs=[pl.BlockSpec((8, D), lambda l, j: (0, 0)),
                  pl.BlockSpec((1, D, tn), lambda l, j: (l, 0, j)),
                  pl.BlockSpec((1, 1, tn), lambda l, j: (l, 0, j))],
        out_specs=pl.BlockSpec((1, 8, tn), lambda l, j: (l, 0, j)),
        out_shape=jax.ShapeDtypeStruct((DEPTH, 8, n), F32),
        compiler_params=_params(("parallel", "parallel")),
        name="modulation",
    )(cvec, w_mod, b_mod.reshape(DEPTH, 1, n))


def _normmod_kernel(x_ref, g_ref, sc_ref, sh_ref, o_ref):
    x = x_ref[...]
    ms = jnp.mean(x * x, axis=-1, keepdims=True)
    y = x * lax.rsqrt(ms + EPS) * g_ref[...]
    o_ref[...] = (y * (1.0 + sc_ref[0]) + sh_ref[0]).astype(o_ref.dtype)


def normmod(x, gain, modtab, k_shift, k_scale, rows):
    tm = 256
    tpb = LX // tm
    return pl.pallas_call(
        _normmod_kernel,
        grid=(rows // tm,),
        in_specs=[pl.BlockSpec((tm, D), lambda i: (i, 0)),
                  pl.BlockSpec((1, D), lambda i: (0, 0)),
                  pl.BlockSpec((1, 1, D), lambda i: (_mod_row(i, tpb) * 6 + k_scale, 0, 0)),
                  pl.BlockSpec((1, 1, D), lambda i: (_mod_row(i, tpb) * 6 + k_shift, 0, 0))],
        out_specs=pl.BlockSpec((tm, D), lambda i: (i, 0)),
        out_shape=jax.ShapeDtypeStruct((rows, D), BF16),
        compiler_params=_params(("parallel",)),
        name="normmod",
    )(x, gain.reshape(1, D), modtab, modtab)


def _mm_kernel(a_ref, w_ref, o_ref, *, act):
    acc = jnp.dot(a_ref[...], w_ref[...], preferred_element_type=F32)
    if act == "sigmoid":
        acc = jax.nn.sigmoid(acc)
    o_ref[...] = acc.astype(o_ref.dtype)


def matmul(a, w, tm, tn, act=None, out_dtype=F32, name="matmul", rows=None):
    m, k = a.shape
    m = rows or m
    n = w.shape[1]
    return pl.pallas_call(
        functools.partial(_mm_kernel, act=act),
        grid=(m // tm, n // tn),
        in_specs=[pl.BlockSpec((tm, k), lambda i, j: (i, 0)),
                  pl.BlockSpec((k, tn), lambda i, j: (0, j))],
        out_specs=pl.BlockSpec((tm, tn), lambda i, j: (i, j)),
        out_shape=jax.ShapeDtypeStruct((m, n), out_dtype),
        compiler_params=_params(("parallel", "parallel")),
        name=name,
    )(a, w)


def _mm_nt_kernel(w_ref, a_ref, o_ref):
    o_ref[...] = lax.dot_general(w_ref[...], a_ref[...], (((1,), (1,)), ((), ())),
                                 preferred_element_type=F32)


def matmul_nt(w_rows, a, tm):
    r, k = w_rows.shape
    m = a.shape[0]
    return pl.pallas_call(
        _mm_nt_kernel,
        grid=(m // tm,),
        in_specs=[pl.BlockSpec((r, k), lambda i: (0, 0)),
                  pl.BlockSpec((tm, k), lambda i: (i, 0))],
        out_specs=pl.BlockSpec((r, tm), lambda i: (0, i)),
        out_shape=jax.ShapeDtypeStruct((r, m), F32),
        compiler_params=_params(("parallel",)),
        name="matmul_nt",
    )(w_rows, a)


CONV_TM = 256
CONV_TC = 2048
CONV_HALO = 8


def _conv_kernel(prev_ref, cur_ref, next_ref, w_ref, b_ref, o_ref, ext_ref):
    i = pl.program_id(0)
    blocks_per_seq = LX // CONV_TM
    is_ctx = i >= NX // CONV_TM
    first = jnp.logical_or(is_ctx, i % blocks_per_seq == 0)
    last = jnp.logical_or(is_ctx, i % blocks_per_seq == blocks_per_seq - 1)
    zeros = jnp.zeros((CONV_HALO, CONV_TC), F32)
    ext_ref[0:CONV_HALO, :] = jnp.where(first, zeros, prev_ref[...])
    ext_ref[CONV_HALO:CONV_HALO + CONV_TM, :] = cur_ref[...]
    ext_ref[CONV_HALO + CONV_TM:, :] = jnp.where(last, zeros, next_ref[...])
    acc = b_ref[...] + jnp.zeros((CONV_TM, CONV_TC), F32)
    for k in range(5):
        acc = acc + ext_ref[pl.ds(CONV_HALO + k - 2, CONV_TM), :] * w_ref[k:k + 1, :]
    o_ref[...] = acc * jax.nn.sigmoid(acc)


def ssd_conv(p_ssd, conv_w, conv_b):
    nblk = NT // CONV_TM
    hb = CONV_TM // CONV_HALO
    c0 = 0
    w8 = jnp.zeros((8, SSD_CONV_CH), F32).at[:5].set(conv_w)
    return pl.pallas_call(
        _conv_kernel,
        grid=(nblk, SSD_CONV_CH // CONV_TC),
        in_specs=[pl.BlockSpec((CONV_HALO, CONV_TC), lambda i, j: (jnp.maximum(i * hb - 1, 0), c0 + j)),
                  pl.BlockSpec((CONV_TM, CONV_TC), lambda i, j: (i, c0 + j)),
                  pl.BlockSpec((CONV_HALO, CONV_TC),
                               lambda i, j: (jnp.minimum((i + 1) * hb, NT // CONV_HALO - 1), c0 + j)),
                  pl.BlockSpec((8, CONV_TC), lambda i, j: (0, j)),
                  pl.BlockSpec((1, CONV_TC), lambda i, j: (0, j))],
        out_specs=pl.BlockSpec((CONV_TM, CONV_TC), lambda i, j: (i, j)),
        out_shape=jax.ShapeDtypeStruct((NT, SSD_CONV_CH), F32),
        scratch_shapes=[pltpu.VMEM((CONV_TM + 2 * CONV_HALO, CONV_TC), F32)],
        compiler_params=_params(("parallel", "parallel")),
        name="ssd_conv",
    )(p_ssd, p_ssd, p_ssd, w8, conv_b.reshape(1, SSD_CONV_CH))


NCH_C = LC // SSD_Q
NCH_X = LX // SSD_Q
SSD_STEPS = NCH_C + NCH_X


def _softplus(x):
    return jnp.maximum(x, 0.0) + jnp.log1p(jnp.exp(-jnp.abs(x)))


def _ssd_one_direction(d, g, xs_ref, b_ref, c_ref, dtc_ref, dtr_ref, pcol, prow, st_ref, y_ref, row, col):
    q = SSD_Q
    gw = SSD_HPG * SSD_P
    xs = xs_ref[:, g * gw:(g + 1) * gw]
    bm = b_ref[:, g * SSD_N:(g + 1) * SSD_N].astype(BF16)
    cm = c_ref[:, g * SSD_N:(g + 1) * SSD_N].astype(BF16)
    tri_l = (col <= row).astype(F32)
    tri_u = (col >= row).astype(F32)
    t_col = tri_l if d == 0 else tri_u
    t_row = tri_u if d == 0 else tri_l
    mask = (col <= row) if d == 0 else (col >= row)
    dt_c = _softplus(dtc_ref[:, g * 128:(g + 1) * 128] + pcol[2 + d:3 + d, :])
    a_c = dt_c * (-jnp.exp(pcol[d:d + 1, :]))
    cs_c = jnp.dot(t_col, a_c, precision=HIGHEST, preferred_element_type=F32)
    dt_r = _softplus(dtr_ref[g * 8:(g + 1) * 8, :] + prow[2 + d])
    a_r = dt_r * (-jnp.exp(prow[d]))
    cs_r = jnp.dot(a_r, t_row, precision=HIGHEST, preferred_element_type=F32)
    edge = q - 1 if d == 0 else 0
    tot_c = cs_c[edge:edge + 1, :]
    tot_r = cs_r[:, edge:edge + 1]

    def lanes(v):
        return jnp.concatenate([jnp.broadcast_to(v[:, h:h + 1], (v.shape[0], SSD_P))
                                for h in range(SSD_HPG)], axis=1)

    dt_b = lanes(dt_c)
    cs_b = lanes(cs_c)
    tot_b = lanes(tot_c)
    xdt = xs * dt_b
    scores = lax.dot_general(cm, bm, (((1,), (1,)), ((), ())), preferred_element_type=F32)
    st = st_ref[d, g]
    y_off = lax.dot_general(cm, st.astype(BF16), (((1,), (1,)), ((), ())),
                            preferred_element_type=F32) * jnp.exp(cs_b)
    xdt16 = xdt.astype(BF16)
    ys = []
    for h in range(SSD_HPG):
        diff = cs_c[:, h:h + 1] - cs_r[h:h + 1, :]
        decay = jnp.exp(jnp.where(mask, diff, -jnp.inf))
        ys.append(jnp.dot((scores * decay).astype(BF16), xdt16[:, h * SSD_P:(h + 1) * SSD_P],
                          preferred_element_type=F32))
    y_ref[:, g * gw:(g + 1) * gw] = jnp.concatenate(ys, axis=1) + y_off
    xw_t = (xdt * jnp.exp(tot_b - cs_b)).T.astype(BF16)
    s_new = jnp.dot(xw_t, bm, preferred_element_type=F32)
    dec = jnp.concatenate([jnp.broadcast_to(jnp.exp(tot_r[h:h + 1, :]), (SSD_P, SSD_N))
                           for h in range(SSD_HPG)], axis=0)
    st_ref[d, g] = st * dec + s_new


def _ssd_kernel(xf, bf, cf, dtcf, dtrf, xr, br, cr, dtcr, dtrr, pcol_ref, prow_ref, yf_ref, yr_ref, st_ref):
    k = pl.program_id(1)

    @pl.when(k == 0)
    def _():
        st_ref[...] = jnp.zeros_like(st_ref)

    row = lax.broadcasted_iota(jnp.int32, (SSD_Q, SSD_Q), 0)
    col = lax.broadcasted_iota(jnp.int32, (SSD_Q, SSD_Q), 1)
    for g in range(SSD_G):
        pcol = pcol_ref[g]
        prow = prow_ref[g]
        _ssd_one_direction(0, g, xf, bf, cf, dtcf, dtrf, pcol, prow, st_ref, yf_ref, row, col)
        _ssd_one_direction(1, g, xr, br, cr, dtcr, dtrr, pcol, prow, st_ref, yr_ref, row, col)


def _chunk_fwd(b, k):
    return jnp.where(k < NCH_C, NX // SSD_Q + b * NCH_C + k, b * NCH_X + (k - NCH_C))


def _chunk_rev(b, k):
    return jnp.where(k < NCH_C, NX // SSD_Q + b * NCH_C + (NCH_C - 1 - k),
                     b * NCH_X + (NCH_X - 1 - (k - NCH_C)))


def ssd_scan(xbc, p_ssd, dt_rows, pcol, prow):
    q = SSD_Q
    dtc0 = (SSD_INNER + SSD_CONV_CH) // 128

    dtcw = SSD_G * 128

    def specs(chunk):
        return [pl.BlockSpec((q, SSD_INNER), lambda b, k: (chunk(b, k), 0)),
                pl.BlockSpec((q, SSD_BC), lambda b, k: (chunk(b, k), SSD_INNER // SSD_BC)),
                pl.BlockSpec((q, SSD_BC), lambda b, k: (chunk(b, k), SSD_INNER // SSD_BC + 1)),
                pl.BlockSpec((q, dtcw), lambda b, k: (chunk(b, k), dtc0 * 128 // dtcw)),
                pl.BlockSpec((SSD_G * 8, q), lambda b, k: (0, chunk(b, k)))]

    def operands():
        return [xbc, xbc, xbc, p_ssd, dt_rows]

    out_spec_f = pl.BlockSpec((q, SSD_INNER), lambda b, k: (_chunk_fwd(b, k), 0))
    out_spec_r = pl.BlockSpec((q, SSD_INNER), lambda b, k: (_chunk_rev(b, k), 0))
    return pl.pallas_call(
        _ssd_kernel,
        grid=(NB, SSD_STEPS),
        in_specs=specs(_chunk_fwd) + specs(_chunk_rev) + [
            pl.BlockSpec((SSD_G, 8, 128), lambda b, k: (0, 0, 0)),
            pl.BlockSpec((SSD_G, 4, 8, 128), lambda b, k: (0, 0, 0, 0))],
        out_specs=[out_spec_f, out_spec_r],
        out_shape=[jax.ShapeDtypeStruct((NT, SSD_INNER), F32)] * 2,
        scratch_shapes=[pltpu.VMEM((2, SSD_G, SSD_HPG * SSD_P, SSD_N), F32)],
        compiler_params=_params(("parallel", "arbitrary")),
        name="ssd_scan",
    )(*operands(), *operands(), pcol, prow)


def _ssd_finish_kernel(yf_ref, yr_ref, xs_ref, z_ref, dsum_ref, g_ref, o_ref):
    z = z_ref[...]
    y = (yf_ref[...] + yr_ref[...] + dsum_ref[...] * xs_ref[...]) * (z * jax.nn.sigmoid(z))
    ms = jnp.mean(y * y, axis=-1, keepdims=True)
    o_ref[...] = (y * lax.rsqrt(ms + EPS) * g_ref[...]).astype(o_ref.dtype)


def ssd_finish(yf, yr, xbc, p_ssd, dsum, gain, rows):
    tm = 512
    blk = lambda: pl.BlockSpec((tm, SSD_INNER), lambda i: (i, 0))
    vec = lambda: pl.BlockSpec((1, SSD_INNER), lambda i: (0, 0))
    return pl.pallas_call(
        _ssd_finish_kernel,
        grid=(rows // tm,),
        in_specs=[blk(), blk(), blk(),
                  pl.BlockSpec((tm, SSD_INNER), lambda i: (i, SSD_CONV_CH // SSD_INNER)),
                  vec(), vec()],
        out_specs=blk(),
        out_shape=jax.ShapeDtypeStruct((rows, SSD_INNER), BF16),
        compiler_params=_params(("parallel",)),
        name="ssd_finish",
    )(yf, yr, xbc, p_ssd, dsum, gain.reshape(1, SSD_INNER))


MLA_TM = 256
LKV = LX + LC
Q_PRESCALE = (MLA_NOPE + MLA_ROPE) ** -0.5 * math.log2(math.e)


def _mla_prep_kernel(p_ref, wq_ref, wkv_ref, gq_ref, gkv_ref, tq1_ref, tq2_ref, tk1_ref, tk2_ref,
                     ct_ref, st_ref, q_ref, k_ref, v_ref):
    lane = lax.broadcasted_iota(jnp.int32, (MLA_TM, 128), 1)
    is_nope = lane < MLA_NOPE
    is_rope = jnp.logical_and(lane >= MLA_NOPE, lane < MLA_NOPE + MLA_ROPE)
    ct = ct_ref[...]
    st = st_ref[...]
    qa = p_ref[:, 0:MLA_QR]
    qn = (qa * lax.rsqrt(jnp.mean(qa * qa, axis=-1, keepdims=True) + EPS) * gq_ref[...]).astype(BF16)
    kva = p_ref[:, MLA_QR:MLA_QR + MLA_KVR]
    kvn = (kva * lax.rsqrt(jnp.mean(kva * kva, axis=-1, keepdims=True) + EPS) * gkv_ref[...]).astype(BF16)
    kp = p_ref[:, MLA_QR + MLA_KVR:]
    ss = jnp.sum(jnp.where(is_rope, kp * kp, 0.0), axis=-1, keepdims=True) * (1.0 / MLA_ROPE)
    r = lax.rsqrt(ss + EPS)
    kpe = kp * (r * ct * tk1_ref[1:2, :]) + pltpu.roll(kp, 96, 1) * (r * st * tk2_ref[...])
    rq_all = jnp.dot(qn, wq_ref[...], preferred_element_type=F32)
    rk_all = jnp.dot(kvn, wkv_ref[...], preferred_element_type=F32)
    tq1c = tq1_ref[...] * ct * Q_PRESCALE
    tq2s = tq2_ref[...] * st * Q_PRESCALE
    for h in range(MLA_H):
        rq = rq_all[:, h * 128:(h + 1) * 128]
        sq = rq * rq
        rn = lax.rsqrt(jnp.sum(jnp.where(is_nope, sq, 0.0), axis=-1, keepdims=True) * (1.0 / MLA_NOPE) + EPS)
        rp = lax.rsqrt(jnp.sum(jnp.where(is_rope, sq, 0.0), axis=-1, keepdims=True) * (1.0 / MLA_ROPE) + EPS)
        q_ref[0, h] = (rq * (jnp.where(is_nope, rn, rp) * tq1c)
                       + pltpu.roll(rq, 96, 1) * (rp * tq2s)).astype(BF16)
        rk = rk_all[:, h * 128:(h + 1) * 128]
        rkn = lax.rsqrt(jnp.sum(jnp.where(is_nope, rk * rk, 0.0), axis=-1, keepdims=True) * (1.0 / MLA_NOPE)
                        + EPS)
        k_ref[0, h] = jnp.where(is_nope, rk * rkn * tk1_ref[0:1, :], kpe).astype(BF16)
        v_ref[0, h] = rk.astype(BF16)


def _mla_tile_b(i):
    return jnp.where(i < NX // MLA_TM, i // (LX // MLA_TM), i - NX // MLA_TM)


def _mla_tile_r(i):
    return jnp.where(i < NX // MLA_TM, i % (LX // MLA_TM), LX // MLA_TM)


def mla_prep(p_mla, wq, wkv, gq, gkv, tq1, tq2, tk1, tk2, ctab, stab):
    ntile = NT // MLA_TM
    n_in = p_mla.shape[1]
    vec = lambda r: pl.BlockSpec((r, 128), lambda i: (0, 0))
    head_out = lambda: pl.BlockSpec((1, MLA_H, MLA_TM, 128), lambda i: (_mla_tile_b(i), 0, _mla_tile_r(i), 0))
    shp = jax.ShapeDtypeStruct((NB, MLA_H, LKV, 128), BF16)
    return pl.pallas_call(
        _mla_prep_kernel,
        grid=(ntile,),
        in_specs=[pl.BlockSpec((MLA_TM, n_in), lambda i: (i, 0)),
                  pl.BlockSpec((MLA_QR, MLA_H * 128), lambda i: (0, 0)),
                  pl.BlockSpec((MLA_KVR, MLA_H * 128), lambda i: (0, 0)),
                  pl.BlockSpec((1, MLA_QR), lambda i: (0, 0)),
                  pl.BlockSpec((1, MLA_KVR), lambda i: (0, 0)),
                  vec(1), vec(1), vec(2), vec(1),
                  pl.BlockSpec((MLA_TM, 128), lambda i: (i, 0)),
                  pl.BlockSpec((MLA_TM, 128), lambda i: (i, 0))],
        out_specs=[head_out(), head_out(), head_out()],
        out_shape=[shp, shp, shp],
        compiler_params=_params(("parallel",)),
        name="mla_prep",
    )(p_mla, wq, wkv, gq, gkv, tq1, tq2, tk1, tk2, ctab, stab)


ATT_TQ = 256
ATT_HEADS = 4


def _attend(q, k, v):
    s = lax.dot_general(q, k, (((1,), (1,)), ((), ())), preferred_element_type=F32)
    m = jnp.max(s, axis=-1, keepdims=True)
    e = jnp.exp2(s - m)
    l = jnp.sum(e, axis=-1, keepdims=True)
    return jnp.dot(e.astype(BF16), v, preferred_element_type=F32) / l


def _attention_kernel(q_ref, k_ref, v_ref, o_ref, *, with_ctx):
    qi = pl.program_id(2)
    lane = lax.broadcasted_iota(jnp.int32, (ATT_TQ, 128), 1)

    def run(lo):
        for p in range(ATT_HEADS // 2):
            o0 = _attend(q_ref[0, 2 * p], k_ref[0, 2 * p, lo:, :], v_ref[0, 2 * p, lo:, :])
            o1 = _attend(q_ref[0, 2 * p + 1], k_ref[0, 2 * p + 1, lo:, :], v_ref[0, 2 * p + 1, lo:, :])
            o_ref[:, p * 128:(p + 1) * 128] = jnp.where(lane < MLA_V, pltpu.roll(o0, 64, 1), o1).astype(o_ref.dtype)

    if with_ctx:
        @pl.when(qi < LX // ATT_TQ)
        def _():
            run(0)

        @pl.when(qi == LX // ATT_TQ)
        def _():
            run(LX)
    else:
        run(0)


def attention(qh, kh, vh, with_ctx):
    nq = LX // ATT_TQ + (1 if with_ctx else 0)
    rows = NT if with_ctx else NX

    def out_row(b, qi):
        return jnp.where(qi < LX // ATT_TQ, b * (LX // ATT_TQ) + qi, NX // ATT_TQ + b)

    nh = ATT_HEADS
    kv_spec = lambda: pl.BlockSpec((1, nh, LKV, 128), lambda b, p, qi: (b, p, 0, 0))
    return pl.pallas_call(
        functools.partial(_attention_kernel, with_ctx=with_ctx),
        grid=(NB, MLA_H // nh, nq),
        in_specs=[pl.BlockSpec((1, nh, ATT_TQ, 128), lambda b, p, qi: (b, p, qi, 0)), kv_spec(), kv_spec()],
        out_specs=pl.BlockSpec((ATT_TQ, nh * MLA_V), lambda b, p, qi: (out_row(b, qi), p)),
        out_shape=jax.ShapeDtypeStruct((rows, MLA_H * MLA_V), BF16),
        compiler_params=_params(("parallel", "parallel", "arbitrary")),
        name="mla_attention",
    )(qh, kh, vh)


S5_RX = NX // S5_T
S5_RC = NC // S5_T
S5_R = S5_RX + S5_RC
S5_W = S5_T * 128
S5_ST = 2 * S5_GB * S5_P


def _s5_scan_rows(s_scr, xin_scr, a_re, a_im, reverse):
    half = S5_ST // 2
    cx = LX // S5_T
    cc = LC // S5_T
    zero = jnp.zeros((1, half), F32)

    def sweep(base_of, n, state):
        def body(i, st):
            c = (n - 1 - i) if reverse else i
            new = []
            for b in range(NB):
                r = base_of(b) + c
                xr, xi = st[b]
                xin_scr[pl.ds(r, 1), :] = jnp.concatenate([xr, xi], axis=1)
                srow = s_scr[pl.ds(r, 1), :]
                new.append((a_re * xr - a_im * xi + srow[:, :half],
                            a_re * xi + a_im * xr + srow[:, half:]))
            return tuple(new)
        return lax.fori_loop(0, n, body, state)

    state = tuple((zero, zero) for _ in range(NB))
    state = sweep(lambda b: S5_RX + b * cc, cc, state)
    sweep(lambda b: b * cx, cx, state)


def _s5_kernel(u_ref, tiles_ref, win_ref, wout_ref, at_ref, y_ref, m_scr, s_scr, xin_scr):
    d = pl.program_id(1)
    ucat = jnp.concatenate([u_ref[pl.ds(t, S5_R, stride=S5_T), :] for t in range(S5_T)],
                           axis=1).astype(BF16)
    s_scr[...] = jnp.dot(ucat, win_ref[0, 0], preferred_element_type=F32)
    a_re = at_ref[0, 0][:, :S5_ST // 2]
    a_im = at_ref[0, 0][:, S5_ST // 2:]
    zero_tile = jnp.zeros((128, 128), BF16)

    def build(reverse):
        for tp in range(S5_T):
            for t in range(S5_T):
                lag = (tp - t) if reverse else (t - tp)
                m_scr[tp * 128:(tp + 1) * 128, t * 128:(t + 1) * 128] = (
                    tiles_ref[0, 0, lag] if lag >= 0 else zero_tile)

    @pl.when(d == 0)
    def _():
        build(False)
        _s5_scan_rows(s_scr, xin_scr, a_re, a_im, False)

    @pl.when(d == 1)
    def _():
        build(True)
        _s5_scan_rows(s_scr, xin_scr, a_re, a_im, True)

    ycat = (jnp.dot(ucat, m_scr[...], preferred_element_type=F32)
            + jnp.dot(xin_scr[...].astype(BF16), wout_ref[0, 0], preferred_element_type=F32))

    @pl.when(d == 0)
    def _():
        for t in range(S5_T):
            y_ref[pl.ds(t, S5_R, stride=S5_T), :] = ycat[:, t * 128:(t + 1) * 128]

    @pl.when(d == 1)
    def _():
        for t in range(S5_T):
            y_ref[pl.ds(t, S5_R, stride=S5_T), :] += ycat[:, t * 128:(t + 1) * 128]


def s5_conv(u, tiles, win, wout, at):
    nblk = S5_INNER // 128
    return pl.pallas_call(
        _s5_kernel,
        grid=(nblk, 2),
        in_specs=[pl.BlockSpec((NT, 128), lambda g, d: (0, g)),
                  pl.BlockSpec((1, 1, S5_T, 128, 128), lambda g, d: (d, g, 0, 0, 0)),
                  pl.BlockSpec((1, 1, S5_W, S5_ST), lambda g, d: (d, g, 0, 0)),
                  pl.BlockSpec((1, 1, S5_ST, S5_W), lambda g, d: (d, g, 0, 0)),
                  pl.BlockSpec((1, 1, 1, S5_ST), lambda g, d: (d, g, 0, 0))],
        out_specs=pl.BlockSpec((NT, 128), lambda g, d: (0, g)),
        out_shape=jax.ShapeDtypeStruct((NT, S5_INNER), F32),
        scratch_shapes=[pltpu.VMEM((S5_W, S5_W), BF16), pltpu.VMEM((S5_R, S5_ST), F32),
                        pltpu.VMEM((S5_R, S5_ST), F32)],
        compiler_params=_params(("parallel", "arbitrary")),
        name="s5_conv",
    )(u, tiles, win, wout, at)


def s5_operators(lam_re, lam_im, log_dt, b_re, b_im, c_re, c_im):
    t_len = S5_T
    dt = jnp.exp(log_dt)[..., None]
    mag = jnp.exp(lam_re * dt)
    ab_re = mag * jnp.cos(lam_im * dt)
    ab_im = mag * jnp.sin(lam_im * dt)
    den = lam_re * lam_re + lam_im * lam_im
    nr = ab_re - 1.0
    f_re = ((nr * lam_re + ab_im * lam_im) / den)[..., None]
    f_im = ((ab_im * lam_re - nr * lam_im) / den)[..., None]
    bb_re = f_re * b_re - f_im * b_im
    bb_im = f_re * b_im + f_im * b_re
    pr, pi = [jnp.ones_like(ab_re)], [jnp.zeros_like(ab_re)]
    for _ in range(t_len):
        pr, pi = pr + [pr[-1] * ab_re - pi[-1] * ab_im], pi + [pr[-1] * ab_im + pi[-1] * ab_re]
    pw_re = jnp.stack(pr)
    pw_im = jnp.stack(pi)
    cp_re = c_re[None] * pw_re[:, :, :, None, :] - c_im[None] * pw_im[:, :, :, None, :]
    cp_im = c_re[None] * pw_im[:, :, :, None, :] + c_im[None] * pw_re[:, :, :, None, :]
    kj = (jnp.einsum("jdgip,dgps->jdgis", cp_re[:t_len], bb_re, precision=HIGHEST)
          - jnp.einsum("jdgip,dgps->jdgis", cp_im[:t_len], bb_im, precision=HIGHEST))
    nb = S5_GROUPS // S5_GB

    def block_diag(small, src_of_col, row_group, col_group):
        c = small.shape[-1]
        place = (jnp.arange(c)[:, None] == src_of_col[None, :]).astype(F32)
        keep = row_group[:, None] == col_group[None, :]
        return jnp.where(keep, jnp.einsum("...rc,cn->...rn", small, place), 0.0)

    q128 = jnp.arange(128)
    q1k = jnp.arange(S5_ST)
    kj = kj.reshape(t_len, 2, nb, S5_GB, S5_S, S5_S).transpose(1, 2, 0, 3, 5, 4)
    tiles = block_diag(kj.reshape(2, nb, t_len, 128, S5_S), q128 % S5_S, q128 // S5_S, q128 // S5_S)
    ab_r = pw_re[:t_len, :, :, :, None] * bb_re[None] - pw_im[:t_len, :, :, :, None] * bb_im[None]
    ab_i = pw_re[:t_len, :, :, :, None] * bb_im[None] + pw_im[:t_len, :, :, :, None] * bb_re[None]
    ab = jnp.stack([ab_r, ab_i], axis=-1)
    ab = jnp.stack([ab[::-1, 0], ab[:, 1]], axis=1)
    ab = ab.reshape(t_len, 2, nb, S5_GB, S5_P, S5_S, 2).transpose(1, 2, 0, 3, 5, 6, 4)
    half = S5_ST // 2
    win = block_diag(ab.reshape(2, nb, t_len * 128, 2 * S5_P),
                     (q1k // half) * S5_P + q1k % S5_P,
                     (jnp.arange(t_len * 128) // S5_S) % S5_GB, (q1k % half) // S5_P)
    q_re = jnp.stack([cp_re[1:, 0], cp_re[:0:-1, 1]], axis=1)
    q_im = jnp.stack([cp_im[1:, 0], cp_im[:0:-1, 1]], axis=1)
    qq = jnp.stack([q_re, -q_im], axis=-1)
    qq = qq.reshape(t_len, 2, nb, S5_GB, S5_S, S5_P, 2).transpose(1, 2, 6, 3, 5, 0, 4)
    qw = jnp.arange(t_len * 128)
    wout = block_diag(qq.reshape(2, nb, S5_ST, t_len * S5_S),
                      (qw // 128) * S5_S + qw % S5_S,
                      (q1k % half) // S5_P, (qw % 128) // S5_S)
    at = jnp.stack([pw_re[t_len], pw_im[t_len]], axis=1)
    at = at.reshape(2, 2, nb, S5_GB * S5_P).transpose(0, 2, 1, 3).reshape(2, nb, 1, S5_ST)
    return tiles.astype(BF16), win.astype(BF16), wout.astype(BF16), at


def _s5_finish_kernel(y_ref, u_ref, d_ref, w_ref, o_ref):
    y = y_ref[...] + d_ref[...] * u_ref[...]
    v = jax.nn.gelu(y)
    gate = jnp.dot(v.astype(BF16), w_ref[...], preferred_element_type=F32)
    o_ref[...] = (v * jax.nn.sigmoid(gate)).astype(o_ref.dtype)


def s5_finish(y, u, d_skip, w_glu, rows):
    tm = 512
    blk = lambda: pl.BlockSpec((tm, S5_INNER), lambda i: (i, 0))
    return pl.pallas_call(
        _s5_finish_kernel,
        grid=(rows // tm,),
        in_specs=[blk(), blk(), pl.BlockSpec((1, S5_INNER), lambda i: (0, 0)),
                  pl.BlockSpec((S5_INNER, S5_INNER), lambda i: (0, 0))],
        out_specs=blk(),
        out_shape=jax.ShapeDtypeStruct((rows, S5_INNER), BF16),
        compiler_params=_params(("parallel",)),
        name="s5_finish",
    )(y, u, d_skip.reshape(1, S5_INNER), w_glu)


def _merge_kernel(ssd_ref, mla_ref, s5_ref, g1_ref, g2_ref, g3_ref, w1_ref, w2_ref, w3_ref, o_ref):
    m = (g1_ref[...] * jnp.dot(ssd_ref[...], w1_ref[...], preferred_element_type=F32)
         + g2_ref[...] * jnp.dot(mla_ref[...], w2_ref[...], preferred_element_type=F32)
         + g3_ref[...] * jnp.dot(s5_ref[...], w3_ref[...], preferred_element_type=F32))
    o_ref[...] = m.astype(o_ref.dtype)


def merge(ssd_o, mla_o, s5_o, gates, w1, w2, w3, rows):
    tm, tn = 1024, 512
    nj = D // tn
    act = lambda: pl.BlockSpec((tm, 1024), lambda i, j: (i, 0))
    wsp = lambda: pl.BlockSpec((1024, tn), lambda i, j: (0, j))
    gsp = lambda k: pl.BlockSpec((tm, tn), lambda i, j: (i, k * nj + j))
    return pl.pallas_call(
        _merge_kernel,
        grid=(rows // tm, nj),
        in_specs=[act(), act(), act(), gsp(0), gsp(1), gsp(2), wsp(), wsp(), wsp()],
        out_specs=pl.BlockSpec((tm, tn), lambda i, j: (i, j)),
        out_shape=jax.ShapeDtypeStruct((rows, D), BF16),
        compiler_params=_params(("parallel", "parallel")),
        name="merge",
    )(ssd_o, mla_o, s5_o, gates, gates, gates, w1, w2, w3)


def _outproj_kernel(m_ref, w_ref, x_ref, g_ref, o_ref):
    o_ref[...] = x_ref[...] + g_ref[0] * jnp.dot(m_ref[...], w_ref[...], preferred_element_type=F32)


def outproj(m, w_out, x, modtab, k_gate, rows):
    tm, tn = 1024, 1024
    tpb = LX // tm
    return pl.pallas_call(
        _outproj_kernel,
        grid=(rows // tm, D // tn),
        in_specs=[pl.BlockSpec((tm, D), lambda i, j: (i, 0)),
                  pl.BlockSpec((D, tn), lambda i, j: (0, j)),
                  pl.BlockSpec((tm, tn), lambda i, j: (i, j)),
                  pl.BlockSpec((1, 1, tn), lambda i, j: (_mod_row(i, tpb) * 6 + k_gate, 0, j))],
        out_specs=pl.BlockSpec((tm, tn), lambda i, j: (i, j)),
        out_shape=jax.ShapeDtypeStruct((rows, D), F32),
        compiler_params=_params(("parallel", "parallel")),
        name="outproj",
    )(m, w_out, x, modtab)


def _router_kernel(x_ref, g_ref, sc_ref, sh_ref, w_ref, h_ref, o_ref):
    x = x_ref[...]
    ms = jnp.mean(x * x, axis=-1, keepdims=True)
    h = x * lax.rsqrt(ms + EPS) * g_ref[...] * (1.0 + sc_ref[0]) + sh_ref[0]
    h_ref[...] = h.astype(BF16)
    logits = lax.dot_general(w_ref[...], h, (((1,), (1,)), ((), ())),
                             precision=HIGHEST, preferred_element_type=F32)
    m = jnp.max(logits, axis=0, keepdims=True)
    e = jnp.exp(logits - m)
    o_ref[...] = e / jnp.sum(e, axis=0, keepdims=True)


def router(x, gain, modtab, w_router_t, rows):
    tm = 256
    tpb = LX // tm
    return pl.pallas_call(
        _router_kernel,
        grid=(rows // tm,),
        in_specs=[pl.BlockSpec((tm, D), lambda i: (i, 0)),
                  pl.BlockSpec((1, D), lambda i: (0, 0)),
                  pl.BlockSpec((1, 1, D), lambda i: (_mod_row(i, tpb) * 6 + 4, 0, 0)),
                  pl.BlockSpec((1, 1, D), lambda i: (_mod_row(i, tpb) * 6 + 3, 0, 0)),
                  pl.BlockSpec((N_EXP, D), lambda i: (0, 0))],
        out_specs=[pl.BlockSpec((tm, D), lambda i: (i, 0)),
                   pl.BlockSpec((N_EXP, tm), lambda i: (0, i))],
        out_shape=[jax.ShapeDtypeStruct((rows, D), BF16), jax.ShapeDtypeStruct((N_EXP, rows), F32)],
        compiler_params=_params(("parallel",)),
        name="moe_router",
    )(x, gain.reshape(1, D), modtab, modtab, w_router_t)


def _prefix_count(mask_f, n):
    row = lax.broadcasted_iota(jnp.int32, (128, 128), 0)
    col = lax.broadcasted_iota(jnp.int32, (128, 128), 1)
    upper = (row < col).astype(BF16)
    offset = jnp.zeros((N_EXP, 1), F32)
    parts = []
    for j in range(n // 128):
        blk = mask_f[:, j * 128:(j + 1) * 128]
        parts.append(jnp.dot(blk.astype(BF16), upper, preferred_element_type=F32) + offset)
        offset = offset + jnp.sum(blk, axis=1, keepdims=True)
    return jnp.concatenate(parts, axis=1)


def _select_kernel(aff_ref, g_ref, pos_ref, *, n, cap):
    aff = aff_ref[...]
    capf = float(cap)

    def count_ge(t):
        return jnp.sum((aff >= t).astype(F32), axis=1, keepdims=True)

    def bisect(_, c):
        lo, hi = c
        mid = 0.5 * (lo + hi)
        ok = count_ge(mid) >= capf
        return jnp.where(ok, mid, lo), jnp.where(ok, hi, mid)

    lo, hi = lax.fori_loop(0, 40, bisect, (jnp.zeros((N_EXP, 1), F32), jnp.full((N_EXP, 1), 2.0, F32)))

    def unfinished(c):
        return jnp.min(c[2]) < 0.5

    def step(c):
        cur, thr, done = c
        cand = jnp.max(jnp.where(aff < cur, aff, -1.0), axis=1, keepdims=True)
        ok = (count_ge(cand) >= capf).astype(F32)
        thr = jnp.where(done > 0.5, thr, cand)
        done = jnp.maximum(done, ok)
        return jnp.where(done > 0.5, cur, cand), thr, done

    _, thr, _ = lax.while_loop(unfinished, step, (hi, lo, jnp.zeros((N_EXP, 1), F32)))
    gt = aff > thr
    eq = (aff == thr).astype(F32)
    need = cap - jnp.sum(gt.astype(F32), axis=1, keepdims=True)
    sel = jnp.logical_or(gt, jnp.logical_and(eq > 0, _prefix_count(eq, n) < need))
    sel_f = sel.astype(F32)
    g_ref[...] = jnp.where(sel, aff, 0.0)
    pos_ref[...] = jnp.where(sel, _prefix_count(sel_f, n), -1.0)


def select(aff_t, n, cap, row0):
    b0 = row0 // n
    spec = lambda: pl.BlockSpec((N_EXP, n), lambda b: (0, b))
    shp = jax.ShapeDtypeStruct((N_EXP, NB * n), F32)
    return pl.pallas_call(
        functools.partial(_select_kernel, n=n, cap=cap),
        grid=(NB,),
        in_specs=[pl.BlockSpec((N_EXP, n), lambda b: (0, b0 + b))],
        out_specs=[spec(), spec()],
        out_shape=[shp, shp],
        compiler_params=_params(("parallel",)),
        name="moe_select",
    )(aff_t)


def _gather_kernel(h_ref, pos_ref, g_ref, xs_ref, gs_ref, *, n, cap):
    pos = pos_ref[0, 0]
    slot = lax.broadcasted_iota(jnp.int32, (cap, n), 0).astype(F32)
    onehot = slot == pos
    xs_ref[0] = jnp.dot(onehot.astype(BF16), h_ref[...], preferred_element_type=F32).astype(BF16)
    gsel = jnp.sum(jnp.where(onehot, g_ref[0, 0], 0.0), axis=1, keepdims=True)
    gs_ref[0] = jnp.broadcast_to(gsel, (cap, 128))


def gather(h2, pos_t, g_t, n, cap, row0):
    b0 = row0 // n
    tab = lambda: pl.BlockSpec((1, 1, 1, n), lambda b, e: (e, b, 0, 0))
    return pl.pallas_call(
        functools.partial(_gather_kernel, n=n, cap=cap),
        grid=(NB, N_EXP),
        in_specs=[pl.BlockSpec((n, D), lambda b, e: (b0 + b, 0)), tab(), tab()],
        out_specs=[pl.BlockSpec((1, cap, D), lambda b, e: (e, b, 0)),
                   pl.BlockSpec((1, cap, 128), lambda b, e: (e, b, 0))],
        out_shape=[jax.ShapeDtypeStruct((N_EXP, NB * cap, D), BF16),
                   jax.ShapeDtypeStruct((N_EXP, NB * cap, 128), F32)],
        compiler_params=_params(("parallel", "arbitrary")),
        name="moe_gather",
    )(h2, pos_t, g_t)


EXP_TF = 512
EXP_TN = 512
EXP_NF = FF // EXP_TF
EXP_NN = D // EXP_TN


def _expert_kernel(*refs, with_ctx):
    if with_ctx:
        xs_ref, xc_ref, gs_ref, gc_ref, wg_ref, wu_ref, wd_ref, ys_ref, yc_ref, hid_ref = refs
    else:
        xs_ref, gs_ref, wg_ref, wu_ref, wd_ref, ys_ref, hid_ref = refs
    f = pl.program_id(1)

    @pl.when(f < EXP_NF)
    def _():
        xs = jnp.concatenate([xs_ref[0], xc_ref[0]], axis=0) if with_ctx else xs_ref[0]
        hg = jnp.dot(xs, wg_ref[0].astype(BF16), preferred_element_type=F32)
        hu = jnp.dot(xs, wu_ref[0].astype(BF16), preferred_element_type=F32)
        hid_ref[f] = (hg * jax.nn.sigmoid(hg) * hu).astype(BF16)

    @pl.when(f >= EXP_NF)
    def _():
        acc = None
        for j in range(EXP_NF):
            part = jnp.dot(hid_ref[j], wd_ref[0, j * EXP_TF:(j + 1) * EXP_TF, :].astype(BF16),
                           preferred_element_type=F32)
            acc = part if acc is None else acc + part
        rx = NB * CAP_X
        ys_ref[0] = (acc[0:rx, :] * gs_ref[0][:, 0:1]).astype(BF16)
        if with_ctx:
            yc_ref[0] = (acc[rx:, :] * gc_ref[0][:, 0:1]).astype(BF16)


def experts(xs, gs, w_gate, w_up, w_down, layer, xc=None, gc=None):
    with_ctx = xc is not None
    rx, rc = NB * CAP_X, NB * CAP_C
    e0 = layer * N_EXP
    up_tile = lambda f: jnp.minimum(f, EXP_NF - 1)
    down_tile = lambda f: jnp.maximum(f - EXP_NF, 0)
    tok = lambda r, w: pl.BlockSpec((1, r, w), lambda e, f: (e, 0, 0))
    out_tile = lambda r: pl.BlockSpec((1, r, EXP_TN), lambda e, f: (e, 0, down_tile(f)))
    in_specs = [tok(rx, D)] + ([tok(rc, D)] if with_ctx else []) + [tok(rx, 128)] + ([tok(rc, 128)] if with_ctx else [])
    in_specs += [pl.BlockSpec((1, D, EXP_TF), lambda e, f: (e0 + e, 0, up_tile(f))),
                 pl.BlockSpec((1, D, EXP_TF), lambda e, f: (e0 + e, 0, up_tile(f))),
                 pl.BlockSpec((1, FF, EXP_TN), lambda e, f: (e0 + e, 0, down_tile(f)))]
    out_specs = [out_tile(rx)] + ([out_tile(rc)] if with_ctx else [])
    out_shape = [jax.ShapeDtypeStruct((N_EXP, rx, D), BF16)] + (
        [jax.ShapeDtypeStruct((N_EXP, rc, D), BF16)] if with_ctx else [])
    args = [xs] + ([xc] if with_ctx else []) + [gs] + ([gc] if with_ctx else []) + [w_gate, w_up, w_down]
    return pl.pallas_call(
        functools.partial(_expert_kernel, with_ctx=with_ctx),
        grid=(N_EXP, EXP_NF + EXP_NN),
        in_specs=in_specs,
        out_specs=out_specs,
        out_shape=out_shape,
        scratch_shapes=[pltpu.VMEM((EXP_NF, rx + (rc if with_ctx else 0), EXP_TF), BF16)],
        compiler_params=_params(("parallel", "arbitrary")),
        name="moe_experts",
    )(*args)


def _combine_kernel(*refs, cap, aliased):
    pos_ref, ys_ref, x_ref, g_ref, o_ref = refs[1:] if aliased else refs
    pos = pos_ref[...]
    tm = pos.shape[0]
    cap_pad = max(cap, 128)
    slot = lax.broadcasted_iota(jnp.int32, (tm, cap_pad), 1).astype(F32)
    acc = None
    for e in range(N_EXP):
        onehot = (pos[:, e:e + 1] == slot).astype(BF16)
        ys = ys_ref[e]
        if cap_pad > cap:
            ys = jnp.concatenate([ys, jnp.zeros((cap_pad - cap, D), BF16)], axis=0)
        part = jnp.dot(onehot, ys, preferred_element_type=F32)
        acc = part if acc is None else acc + part
    o_ref[...] = x_ref[...] + g_ref[0] * acc


def combine(pos_col, ys, x, modtab, k_gate, n, cap, row0, mod_row_of_batch, out_rows, prev=None):
    tm = 256
    tpb = n // tm
    r0 = row0 // tm
    aliased = prev is not None
    in_specs = [pl.BlockSpec((tm, N_EXP), lambda b, t: (b * tpb + t, 0)),
                pl.BlockSpec((N_EXP, cap, D), lambda b, t: (0, b, 0)),
                pl.BlockSpec((tm, D), lambda b, t: (r0 + b * tpb + t, 0)),
                pl.BlockSpec((1, 1, D), lambda b, t: (mod_row_of_batch(b) * 6 + k_gate, 0, 0))]
    args = [pos_col, ys, x, modtab]
    if aliased:
        in_specs = [pl.BlockSpec(memory_space=pl.ANY)] + in_specs
        args = [prev] + args
    return pl.pallas_call(
        functools.partial(_combine_kernel, cap=cap, aliased=aliased),
        grid=(NB, tpb),
        in_specs=in_specs,
        out_specs=pl.BlockSpec((tm, D), lambda b, t: (r0 + b * tpb + t, 0)),
        out_shape=jax.ShapeDtypeStruct((out_rows, D), F32),
        input_output_aliases={0: 0} if aliased else {},
        compiler_params=_params(("parallel", "parallel")),
        name="moe_combine",
    )(*args)


def moe_layer(xa, gain, modtab, w_router_t, w_gate, w_up, w_down, layer, with_ctx):
    rows = NT if with_ctx else NX
    h2, aff = router(xa, gain, modtab, w_router_t, rows)

    def route(n, cap, row0):
        g_t, pos_t = select(aff, n, cap, row0)
        xs, gs = gather(h2, pos_t.reshape(N_EXP, NB, 1, n), g_t.reshape(N_EXP, NB, 1, n), n, cap, row0)
        return pos_t.T, xs, gs

    pos_x, xs_x, gs_x = route(LX, CAP_X, 0)
    if with_ctx:
        pos_c, xs_c, gs_c = route(LC, CAP_C, NX)
        ys_x, ys_c = experts(xs_x, gs_x, w_gate, w_up, w_down, layer, xs_c, gs_c)
    else:
        (ys_x,) = experts(xs_x, gs_x, w_gate, w_up, w_down, layer)
    out = combine(pos_x, ys_x, xa, modtab, 5, LX, CAP_X, 0, lambda b: b, rows)
    if with_ctx:
        out = combine(pos_c, ys_c, xa, modtab, 5, LC, CAP_C, NX, lambda b: NB, rows, prev=out)
    return out


def _rope_tables():
    n_freq = MLA_ROPE // 4
    inv = ROPE_BASE ** (-jnp.arange(n_freq, dtype=F32) / n_freq)
    rows = jnp.repeat(jnp.arange(LX // GRID_W, dtype=F32), GRID_W)
    cols = jnp.tile(jnp.arange(GRID_W, dtype=F32), LX // GRID_W)
    ang = jnp.concatenate([rows[:, None] * inv, cols[:, None] * inv], axis=-1)
    cos, sin = jnp.cos(ang), jnp.sin(ang)
    one = jnp.ones((LX, MLA_NOPE), F32)
    zero = jnp.zeros((LX, 32), F32)
    ct_x = jnp.concatenate([one, cos, cos, zero], axis=1)
    st_x = jnp.concatenate([0 * one, -sin, sin, zero], axis=1)
    ct_c = jnp.concatenate([jnp.ones((NC, 96), F32), jnp.zeros((NC, 32), F32)], axis=1)
    ctab = jnp.concatenate([jnp.tile(ct_x, (NB, 1)), ct_c], axis=0)
    stab = jnp.concatenate([jnp.tile(st_x, (NB, 1)), jnp.zeros((NC, 128), F32)], axis=0)
    return ctab, stab


_PERM = list(range(0, MLA_ROPE, 2)) + list(range(1, MLA_ROPE, 2))
_PERM_SW = list(range(1, MLA_ROPE, 2)) + list(range(0, MLA_ROPE, 2))


def _lane_table(first64, rope32):
    return jnp.concatenate([first64, rope32, jnp.zeros((32,), F32)]).reshape(1, 128)


def _layer_weights(i, w_in, ssd_a_log, ssd_dt_bias, mla_w_q_b, mla_w_kv_b, mla_q_gain, mla_k_gain):
    w = w_in[i]
    wdt = w[:, SSD_INNER + SSD_CONV_CH:O1].reshape(D, SSD_G, SSD_HPG)
    wdt_col = jnp.pad(wdt, ((0, 0), (0, 0), (0, 128 - SSD_HPG))).reshape(D, SSD_G * 128)
    w_ssd = jnp.concatenate([w[:, SSD_INNER:SSD_INNER + SSD_CONV_CH], w[:, :SSD_INNER], wdt_col],
                            axis=1).astype(BF16)
    wdt_row = jnp.pad(jnp.transpose(wdt, (1, 2, 0)), ((0, 0), (0, 8 - SSD_HPG), (0, 0))).reshape(SSD_G * 8, D)
    kpe = w[:, O1 + MLA_QR + MLA_KVR:O2]
    kpe128 = jnp.concatenate([jnp.zeros((D, MLA_NOPE), F32), kpe[:, _PERM], kpe[:, _PERM_SW]], axis=1)
    w_mla = jnp.concatenate([w[:, O1:O1 + MLA_QR + MLA_KVR], kpe128], axis=1).astype(BF16)
    w_s5 = w[:, O2:O3].astype(BF16)
    w_gate = w[:, O3:].astype(BF16)
    par = jnp.stack([ssd_a_log[i, 0], ssd_a_log[i, 1], ssd_dt_bias[i, 0], ssd_dt_bias[i, 1]])
    par = par.reshape(4, SSD_G, SSD_HPG).transpose(1, 0, 2)
    pcol = jnp.pad(par, ((0, 0), (0, 4), (0, 128 - SSD_HPG)))
    prow = jnp.broadcast_to(jnp.pad(par, ((0, 0), (0, 0), (0, 8 - SSD_HPG)))[..., None], (SSD_G, 4, 8, 128))
    wq = mla_w_q_b[i].reshape(MLA_QR, MLA_H, MLA_NOPE + MLA_ROPE)
    wq = jnp.concatenate([wq[..., :MLA_NOPE], wq[..., MLA_NOPE:][..., _PERM], wq[..., MLA_NOPE:][..., _PERM_SW]],
                         axis=-1).reshape(MLA_QR, MLA_H * 128).astype(BF16)
    wkv = mla_w_kv_b[i].astype(BF16)
    qg, kg = mla_q_gain[i], mla_k_gain[i]
    tq1 = _lane_table(qg[:MLA_NOPE], qg[MLA_NOPE:][jnp.array(_PERM)])
    tq2 = _lane_table(jnp.zeros((MLA_NOPE,), F32), qg[MLA_NOPE:][jnp.array(_PERM_SW)])
    tk1 = jnp.concatenate([_lane_table(kg[:MLA_NOPE], jnp.zeros((MLA_ROPE,), F32)),
                           _lane_table(jnp.zeros((MLA_NOPE,), F32), kg[MLA_NOPE:][jnp.array(_PERM)])], axis=0)
    tk2 = _lane_table(jnp.zeros((MLA_NOPE,), F32), kg[MLA_NOPE:][jnp.array(_PERM_SW)])
    return dict(w_ssd=w_ssd, wdt_row=wdt_row.astype(BF16), w_mla=w_mla, w_s5=w_s5, w_gate=w_gate,
                pcol=pcol, prow=prow, wq=wq, wkv=wkv, tq1=tq1, tq2=tq2, tk1=tk1, tk2=tk2)


def kernel(x, c, ctx, c_ctx, norm1_gain, norm2_gain, w_mod, b_mod, w_in, ssd_conv_w, ssd_conv_b, ssd_a_log, ssd_dt_bias, ssd_d, ssd_norm_gain, mla_q_a_gain, mla_kv_a_gain, mla_w_q_b, mla_w_kv_b, mla_q_gain, mla_k_gain, s5_lam_re, s5_lam_im, s5_log_dt, s5_b_re, s5_b_im, s5_c_re, s5_c_im, s5_d, s5_w_glu, w_branch_ssd, w_branch_mla, w_branch_s5, w_out, moe_router, moe_w_gate, moe_w_up, moe_w_down):
    ctab, stab = _rope_tables()
    cvec = jnp.concatenate([c, c_ctx[None], jnp.zeros((8 - NB - 1, D), F32)], axis=0)
    mod = modulation(cvec, w_mod, b_mod)
    xa = jnp.concatenate([x.reshape(NX, D), ctx.reshape(NC, D)], axis=0)
    for i in range(DEPTH):
        need_ctx = i < DEPTH - 1
        rows = NT if need_ctx else NX
        modtab = mod[i].reshape(8 * 6, 1, D)
        lw = _layer_weights(i, w_in, ssd_a_log, ssd_dt_bias, mla_w_q_b, mla_w_kv_b, mla_q_gain, mla_k_gain)
        h = normmod(xa, norm1_gain[i], modtab, 0, 1, NT)
        p_ssd = matmul(h, lw["w_ssd"], 1024, 512, name="inproj_ssd")
        dt_rows = matmul_nt(lw["wdt_row"], h, 1024)
        p_mla = matmul(h, lw["w_mla"], 1024, 896, name="inproj_mla")
        u_s5 = matmul(h, lw["w_s5"], 1024, 512, name="inproj_s5")
        gates = matmul(h, lw["w_gate"], 1024, 1024, act="sigmoid", out_dtype=BF16, name="inproj_gate", rows=rows)
        xbc = ssd_conv(p_ssd, ssd_conv_w[i], ssd_conv_b[i])
        yf, yr = ssd_scan(xbc, p_ssd, dt_rows, lw["pcol"], lw["prow"])
        dsum = jnp.repeat(ssd_d[i, 0] + ssd_d[i, 1], SSD_P).reshape(1, SSD_INNER)
        ssd_o = ssd_finish(yf, yr, xbc, p_ssd, dsum, ssd_norm_gain[i], rows)
        qh, kh, vh = mla_prep(p_mla, lw["wq"], lw["wkv"], mla_q_a_gain[i].reshape(1, MLA_QR),
                              mla_kv_a_gain[i].reshape(1, MLA_KVR), lw["tq1"], lw["tq2"], lw["tk1"], lw["tk2"],
                              ctab, stab)
        mla_o = attention(qh, kh, vh, need_ctx)
        ops = s5_operators(s5_lam_re[i], s5_lam_im[i], s5_log_dt[i], s5_b_re[i], s5_b_im[i], s5_c_re[i], s5_c_im[i])
        y_s5 = s5_conv(u_s5, *ops)
        s5_o = s5_finish(y_s5, u_s5, s5_d[i], s5_w_glu[i].astype(BF16), rows)
        m = merge(ssd_o, mla_o, s5_o, gates, w_branch_ssd[i].astype(BF16), w_branch_mla[i].astype(BF16),
                  w_branch_s5[i].astype(BF16), rows)
        xa = outproj(m, w_out[i].astype(BF16), xa, modtab, 2, rows)
        xa = moe_layer(xa, norm2_gain[i], modtab, moe_router[i].T, moe_w_gate.reshape(DEPTH * N_EXP, D, FF),
                       moe_w_up.reshape(DEPTH * N_EXP, D, FF), moe_w_down.reshape(DEPTH * N_EXP, FF, D), i, need_ctx)
    return xa[:NX].reshape(NB, LX, D)
```

```python
import functools
import math

import jax
import jax.numpy as jnp
from jax import lax
from jax.experimental import pallas as pl
from jax.experimental.pallas import tpu as pltpu

F32 = jnp.float32
BF16 = jnp.bfloat16
HIGHEST = lax.Precision.HIGHEST

D = 2048
NB = 4
LX = 2048
LC = 256
NX = NB * LX
NC = NB * LC
NT = NX + NC
DEPTH = 2
EPS = 1e-6
GRID_W = 64

SSD_HEADS = 16
SSD_P = 64
SSD_INNER = 1024
SSD_G = 4
SSD_HPG = 4
SSD_N = 128
SSD_Q = 128
SSD_BC = SSD_G * SSD_N
SSD_CONV_CH = SSD_INNER + 2 * SSD_BC
SSD_IN = SSD_INNER + SSD_CONV_CH + SSD_HEADS

MLA_H = 16
MLA_QR = 512
MLA_KVR = 256
MLA_NOPE = 64
MLA_ROPE = 32
MLA_V = 64
MLA_IN = MLA_QR + MLA_KVR + MLA_ROPE
ROPE_BASE = 10000.0

S5_INNER = 1024
S5_S = 16
S5_GROUPS = 64
S5_P = 64
S5_T = 8
S5_GB = 8

N_EXP = 16
FF = 2048
CAP_X = 2 * LX // N_EXP
CAP_C = 2 * LC // N_EXP

O1 = SSD_IN
O2 = O1 + MLA_IN
O3 = O2 + S5_INNER

VMEM_LIMIT = 56 * 1024 * 1024


def _params(sem):
    return pltpu.CompilerParams(dimension_semantics=sem, vmem_limit_bytes=VMEM_LIMIT)


def _mod_row(i, tiles_per_batch):
    return jnp.minimum(i // tiles_per_batch, NB)


def _modulation_kernel(c_ref, w_ref, b_ref, o_ref):
    c = c_ref[...]
    a = (c * jax.nn.sigmoid(c)).astype(BF16)
    o_ref[0] = jnp.dot(a, w_ref[0].astype(BF16), preferred_element_type=F32) + b_ref[0]


def modulation(cvec, w_mod, b_mod):
    tn = 1536
    n = w_mod.shape[-1]
    return pl.pallas_call(
        _modulation_kernel,
        grid=(DEPTH, n // tn),
        in_specs=[pl.BlockSpec((8, D), lambda l, j: (0, 0)),
                  pl.BlockSpec((1, D, tn), lambda l, j: (l, 0, j)),
                  pl.BlockSpec((1, 1, tn), lambda l, j: (l, 0, j))],
        out_specs=pl.BlockSpec((1, 8, tn), lambda l, j: (l, 0, j)),
        out_shape=jax.ShapeDtypeStruct((DEPTH, 8, n), F32),
        compiler_params=_params(("parallel", "parallel")),
        name="modulation",
    )(cvec, w_mod, b_mod.reshape(DEPTH, 1, n))


def _normmod_kernel(x_ref, g_ref, sc_ref, sh_ref, o_ref):
    x = x_ref[...]
    ms = jnp.mean(x * x, axis=-1, keepdims=True)
    y = x * lax.rsqrt(ms + EPS) * g_ref[...]
    o_ref[...] = (y * (1.0 + sc_ref[0]) + sh_ref[0]).astype(o_ref.dtype)


def normmod(x, gain, modtab, k_shift, k_scale, rows):
    tm = 256
    tpb = LX // tm
    return pl.pallas_call(
        _normmod_kernel,
        grid=(rows // tm,),
        in_specs=[pl.BlockSpec((tm, D), lambda i: (i, 0)),
                  pl.BlockSpec((1, D), lambda i: (0, 0)),
                  pl.BlockSpec((1, 1, D), lambda i: (_mod_row(i, tpb) * 6 + k_scale, 0, 0)),
                  pl.BlockSpec((1, 1, D), lambda i: (_mod_row(i, tpb) * 6 + k_shift, 0, 0))],
        out_specs=pl.BlockSpec((tm, D), lambda i: (i, 0)),
        out_shape=jax.ShapeDtypeStruct((rows, D), BF16),
        compiler_params=_params(("parallel",)),
        name="normmod",
    )(x, gain.reshape(1, D), modtab, modtab)


def _mm_kernel(a_ref, w_ref, o_ref, *, act):
    acc = jnp.dot(a_ref[...], w_ref[...], preferred_element_type=F32)
    if act == "sigmoid":
        acc = jax.nn.sigmoid(acc)
    o_ref[...] = acc.astype(o_ref.dtype)


def matmul(a, w, tm, tn, act=None, out_dtype=F32, name="matmul", rows=None):
    m, k = a.shape
    m = rows or m
    n = w.shape[1]
    return pl.pallas_call(
        functools.partial(_mm_kernel, act=act),
        grid=(m // tm, n // tn),
        in_specs=[pl.BlockSpec((tm, k), lambda i, j: (i, 0)),
                  pl.BlockSpec((k, tn), lambda i, j: (0, j))],
        out_specs=pl.BlockSpec((tm, tn), lambda i, j: (i, j)),
        out_shape=jax.ShapeDtypeStruct((m, n), out_dtype),
        compiler_params=_params(("parallel", "parallel")),
        name=name,
    )(a, w)


def _mm_nt_kernel(w_ref, a_ref, o_ref):
    o_ref[...] = lax.dot_general(w_ref[...], a_ref[...], (((1,), (1,)), ((), ())),
                                 preferred_element_type=F32)


def matmul_nt(w_rows, a, tm):
    r, k = w_rows.shape
    m = a.shape[0]
    return pl.pallas_call(
        _mm_nt_kernel,
        grid=(m // tm,),
        in_specs=[pl.BlockSpec((r, k), lambda i: (0, 0)),
                  pl.BlockSpec((tm, k), lambda i: (i, 0))],
        out_specs=pl.BlockSpec((r, tm), lambda i: (0, i)),
        out_shape=jax.ShapeDtypeStruct((r, m), F32),
        compiler_params=_params(("parallel",)),
        name="matmul_nt",
    )(w_rows, a)


CONV_TM = 256
CONV_TC = 2048
CONV_HALO = 8


def _conv_kernel(prev_ref, cur_ref, next_ref, w_ref, b_ref, o_ref, ext_ref):
    i = pl.program_id(0)
    blocks_per_seq = LX // CONV_TM
    is_ctx = i >= NX // CONV_TM
    first = jnp.logical_or(is_ctx, i % blocks_per_seq == 0)
    last = jnp.logical_or(is_ctx, i % blocks_per_seq == blocks_per_seq - 1)
    zeros = jnp.zeros((CONV_HALO, CONV_TC), F32)
    ext_ref[0:CONV_HALO, :] = jnp.where(first, zeros, prev_ref[...])
    ext_ref[CONV_HALO:CONV_HALO + CONV_TM, :] = cur_ref[...]
    ext_ref[CONV_HALO + CONV_TM:, :] = jnp.where(last, zeros, next_ref[...])
    acc = b_ref[...] + jnp.zeros((CONV_TM, CONV_TC), F32)
    for k in range(5):
        acc = acc + ext_ref[pl.ds(CONV_HALO + k - 2, CONV_TM), :] * w_ref[k:k + 1, :]
    o_ref[...] = acc * jax.nn.sigmoid(acc)


def ssd_conv(p_ssd, conv_w, conv_b):
    nblk = NT // CONV_TM
    hb = CONV_TM // CONV_HALO
    c0 = 0
    w8 = jnp.zeros((8, SSD_CONV_CH), F32).at[:5].set(conv_w)
    return pl.pallas_call(
        _conv_kernel,
        grid=(nblk, SSD_CONV_CH // CONV_TC),
        in_specs=[pl.BlockSpec((CONV_HALO, CONV_TC), lambda i, j: (jnp.maximum(i * hb - 1, 0), c0 + j)),
                  pl.BlockSpec((CONV_TM, CONV_TC), lambda i, j: (i, c0 + j)),
                  pl.BlockSpec((CONV_HALO, CONV_TC),
                               lambda i, j: (jnp.minimum((i + 1) * hb, NT // CONV_HALO - 1), c0 + j)),
                  pl.BlockSpec((8, CONV_TC), lambda i, j: (0, j)),
                  pl.BlockSpec((1, CONV_TC), lambda i, j: (0, j))],
        out_specs=pl.BlockSpec((CONV_TM, CONV_TC), lambda i, j: (i, j)),
        out_shape=jax.ShapeDtypeStruct((NT, SSD_CONV_CH), F32),
        scratch_shapes=[pltpu.VMEM((CONV_TM + 2 * CONV_HALO, CONV_TC), F32)],
        compiler_params=_params(("parallel", "parallel")),
        name="ssd_conv",
    )(p_ssd, p_ssd, p_ssd, w8, conv_b.reshape(1, SSD_CONV_CH))


NCH_C = LC // SSD_Q
NCH_X = LX // SSD_Q
SSD_STEPS = NCH_C + NCH_X


def _softplus(x):
    return jnp.maximum(x, 0.0) + jnp.log1p(jnp.exp(-jnp.abs(x)))


def _split3(a):
    p1 = a.astype(BF16).astype(F32)
    r = a - p1
    p2 = r.astype(BF16).astype(F32)
    p3 = (r - p2).astype(BF16).astype(F32)
    return p1, p2, p3


def _ssd_one_direction(d, g, xs_ref, b_ref, c_ref, dtc_ref, dtr_ref, pcol, prow, st_ref, y_ref, row, col):
    q = SSD_Q
    gw = SSD_HPG * SSD_P
    xs = xs_ref[:, g * gw:(g + 1) * gw]
    bm = b_ref[:, g * SSD_N:(g + 1) * SSD_N].astype(BF16)
    cm = c_ref[:, g * SSD_N:(g + 1) * SSD_N].astype(BF16)
    tri_l = (col <= row).astype(F32)
    tri_u = (col >= row).astype(F32)
    t_col = tri_l if d == 0 else tri_u
    t_row = tri_u if d == 0 else tri_l
    mask = (col <= row) if d == 0 else (col >= row)
    dt_c = _softplus(dtc_ref[:, g * 128:(g + 1) * 128] + pcol[2 + d:3 + d, :])
    a_c = dt_c * (-jnp.exp(pcol[d:d + 1, :]))
    t_col16 = t_col.astype(BF16)
    c1, c2, c3 = (jnp.dot(t_col16, p.astype(BF16), preferred_element_type=F32) for p in _split3(a_c))
    cs_c = c1 + c2 + c3
    sub = lax.broadcasted_iota(jnp.int32, (8, q), 0)
    dt_r = _softplus(dtr_ref[g * 8:(g + 1) * 8, :] + prow[2 + d][:, 0:1])
    a_r = jnp.where(sub < SSD_HPG, dt_r * (-jnp.exp(prow[d][:, 0:1])), 0.0)
    stacked = jnp.concatenate(_split3(a_r), axis=0).astype(BF16)
    sums_r = jnp.dot(stacked, t_row.astype(BF16), preferred_element_type=F32)
    cs_r = sums_r[0:8] + sums_r[8:16] + sums_r[16:24]
    edge = q - 1 if d == 0 else 0
    tot_c = cs_c[edge:edge + 1, :]
    tot_r = cs_r[:, edge:edge + 1]

    def lanes(v):
        return jnp.concatenate([jnp.broadcast_to(v[:, h:h + 1], (v.shape[0], SSD_P))
                                for h in range(SSD_HPG)], axis=1)

    dt_b = lanes(dt_c)
    cs_b = lanes(cs_c)
    tot_b = lanes(tot_c)
    xdt = xs * dt_b
    scores = lax.dot_general(cm, bm, (((1,), (1,)), ((), ())), preferred_element_type=F32)
    st = st_ref[d, g]
    y_off = lax.dot_general(cm, st.astype(BF16), (((1,), (1,)), ((), ())),
                            preferred_element_type=F32) * jnp.exp(cs_b)
    xdt16 = xdt.astype(BF16)
    ys = []
    for h in range(SSD_HPG):
        diff = cs_c[:, h:h + 1] - cs_r[h:h + 1, :]
        decay = jnp.exp(jnp.where(mask, diff, -jnp.inf))
        ys.append(jnp.dot((scores * decay).astype(BF16), xdt16[:, h * SSD_P:(h + 1) * SSD_P],
                          preferred_element_type=F32))
    y_ref[:, g * gw:(g + 1) * gw] = jnp.concatenate(ys, axis=1) + y_off
    xw_t = (xdt * jnp.exp(tot_b - cs_b)).T.astype(BF16)
    s_new = jnp.dot(xw_t, bm, preferred_element_type=F32)
    dec = jnp.concatenate([jnp.broadcast_to(jnp.exp(tot_r[h:h + 1, :]), (SSD_P, SSD_N))
                           for h in range(SSD_HPG)], axis=0)
    st_ref[d, g] = st * dec + s_new


def _ssd_kernel(xf, bf, cf, dtcf, dtrf, xr, br, cr, dtcr, dtrr, pcol_ref, prow_ref, yf_ref, yr_ref, st_ref):
    k = pl.program_id(1)

    @pl.when(k == 0)
    def _():
        st_ref[...] = jnp.zeros_like(st_ref)

    row = lax.broadcasted_iota(jnp.int32, (SSD_Q, SSD_Q), 0)
    col = lax.broadcasted_iota(jnp.int32, (SSD_Q, SSD_Q), 1)
    for g in range(SSD_G):
        pcol = pcol_ref[g]
        prow = prow_ref[g]
        _ssd_one_direction(0, g, xf, bf, cf, dtcf, dtrf, pcol, prow, st_ref, yf_ref, row, col)
        _ssd_one_direction(1, g, xr, br, cr, dtcr, dtrr, pcol, prow, st_ref, yr_ref, row, col)


def _chunk_fwd(b, k):
    return jnp.where(k < NCH_C, NX // SSD_Q + b * NCH_C + k, b * NCH_X + (k - NCH_C))


def _chunk_rev(b, k):
    return jnp.where(k < NCH_C, NX // SSD_Q + b * NCH_C + (NCH_C - 1 - k),
                     b * NCH_X + (NCH_X - 1 - (k - NCH_C)))


def ssd_scan(xbc, p_ssd, dt_rows, pcol, prow):
    q = SSD_Q
    dtc0 = (SSD_INNER + SSD_CONV_CH) // 128

    dtcw = SSD_G * 128

    def specs(chunk):
        return [pl.BlockSpec((q, SSD_INNER), lambda b, k: (chunk(b, k), 0)),
                pl.BlockSpec((q, SSD_BC), lambda b, k: (chunk(b, k), SSD_INNER // SSD_BC)),
                pl.BlockSpec((q, SSD_BC), lambda b, k: (chunk(b, k), SSD_INNER // SSD_BC + 1)),
                pl.BlockSpec((q, dtcw), lambda b, k: (chunk(b, k), dtc0 * 128 // dtcw)),
                pl.BlockSpec((SSD_G * 8, q), lambda b, k: (0, chunk(b, k)))]

    def operands():
        return [xbc, xbc, xbc, p_ssd, dt_rows]

    out_spec_f = pl.BlockSpec((q, SSD_INNER), lambda b, k: (_chunk_fwd(b, k), 0))
    out_spec_r = pl.BlockSpec((q, SSD_INNER), lambda b, k: (_chunk_rev(b, k), 0))
    return pl.pallas_call(
        _ssd_kernel,
        grid=(NB, SSD_STEPS),
        in_specs=specs(_chunk_fwd) + specs(_chunk_rev) + [
            pl.BlockSpec((SSD_G, 8, 128), lambda b, k: (0, 0, 0)),
            pl.BlockSpec((SSD_G, 4, 8, 128), lambda b, k: (0, 0, 0, 0))],
        out_specs=[out_spec_f, out_spec_r],
        out_shape=[jax.ShapeDtypeStruct((NT, SSD_INNER), F32)] * 2,
        scratch_shapes=[pltpu.VMEM((2, SSD_G, SSD_HPG * SSD_P, SSD_N), F32)],
        compiler_params=_params(("parallel", "arbitrary")),
        name="ssd_scan",
    )(*operands(), *operands(), pcol, prow)


def _ssd_finish_kernel(yf_ref, yr_ref, xs_ref, z_ref, dsum_ref, g_ref, o_ref):
    z = z_ref[...]
    y = (yf_ref[...] + yr_ref[...] + dsum_ref[...] * xs_ref[...]) * (z * jax.nn.sigmoid(z))
    ms = jnp.mean(y * y, axis=-1, keepdims=True)
    o_ref[...] = (y * lax.rsqrt(ms + EPS) * g_ref[...]).astype(o_ref.dtype)


def ssd_finish(yf, yr, xbc, p_ssd, dsum, gain, rows):
    tm = 512
    blk = lambda: pl.BlockSpec((tm, SSD_INNER), lambda i: (i, 0))
    vec = lambda: pl.BlockSpec((1, SSD_INNER), lambda i: (0, 0))
    return pl.pallas_call(
        _ssd_finish_kernel,
        grid=(rows // tm,),
        in_specs=[blk(), blk(), blk(),
                  pl.BlockSpec((tm, SSD_INNER), lambda i: (i, SSD_CONV_CH // SSD_INNER)),
                  vec(), vec()],
        out_specs=blk(),
        out_shape=jax.ShapeDtypeStruct((rows, SSD_INNER), BF16),
        compiler_params=_params(("parallel",)),
        name="ssd_finish",
    )(yf, yr, xbc, p_ssd, dsum, gain.reshape(1, SSD_INNER))


MLA_TM = 256
LKV = LX + LC
Q_PRESCALE = (MLA_NOPE + MLA_ROPE) ** -0.5 * math.log2(math.e)


def _mla_prep_kernel(p_ref, wq_ref, wkv_ref, gq_ref, gkv_ref, tq1_ref, tq2_ref, tk1_ref, tk2_ref,
                     ct_ref, st_ref, q_ref, k_ref, v_ref):
    lane = lax.broadcasted_iota(jnp.int32, (MLA_TM, 128), 1)
    is_nope = lane < MLA_NOPE
    is_rope = jnp.logical_and(lane >= MLA_NOPE, lane < MLA_NOPE + MLA_ROPE)
    ct = ct_ref[...]
    st = st_ref[...]
    qa = p_ref[:, 0:MLA_QR]
    qn = (qa * lax.rsqrt(jnp.mean(qa * qa, axis=-1, keepdims=True) + EPS) * gq_ref[...]).astype(BF16)
    kva = p_ref[:, MLA_QR:MLA_QR + MLA_KVR]
    kvn = (kva * lax.rsqrt(jnp.mean(kva * kva, axis=-1, keepdims=True) + EPS) * gkv_ref[...]).astype(BF16)
    kp = p_ref[:, MLA_QR + MLA_KVR:]
    ss = jnp.sum(jnp.where(is_rope, kp * kp, 0.0), axis=-1, keepdims=True) * (1.0 / MLA_ROPE)
    r = lax.rsqrt(ss + EPS)
    kpe = kp * (r * ct * tk1_ref[1:2, :]) + pltpu.roll(kp, 96, 1) * (r * st * tk2_ref[...])
    rq_all = jnp.dot(qn, wq_ref[...], preferred_element_type=F32)
    rk_all = jnp.dot(kvn, wkv_ref[...], preferred_element_type=F32)
    tq1c = tq1_ref[...] * ct * Q_PRESCALE
    tq2s = tq2_ref[...] * st * Q_PRESCALE
    for h in range(MLA_H):
        rq = rq_all[:, h * 128:(h + 1) * 128]
        sq = rq * rq
        rn = lax.rsqrt(jnp.sum(jnp.where(is_nope, sq, 0.0), axis=-1, keepdims=True) * (1.0 / MLA_NOPE) + EPS)
        rp = lax.rsqrt(jnp.sum(jnp.where(is_rope, sq, 0.0), axis=-1, keepdims=True) * (1.0 / MLA_ROPE) + EPS)
        q_ref[0, h] = (rq * (jnp.where(is_nope, rn, rp) * tq1c)
                       + pltpu.roll(rq, 96, 1) * (rp * tq2s)).astype(BF16)
        rk = rk_all[:, h * 128:(h + 1) * 128]
        rkn = lax.rsqrt(jnp.sum(jnp.where(is_nope, rk * rk, 0.0), axis=-1, keepdims=True) * (1.0 / MLA_NOPE)
                        + EPS)
        k_ref[0, h] = jnp.where(is_nope, rk * rkn * tk1_ref[0:1, :], kpe).astype(BF16)
        v_ref[0, h] = rk.astype(BF16)


def _mla_tile_b(i):
    return jnp.where(i < NX // MLA_TM, i // (LX // MLA_TM), i - NX // MLA_TM)


def _mla_tile_r(i):
    return jnp.where(i < NX // MLA_TM, i % (LX // MLA_TM), LX // MLA_TM)


def mla_prep(p_mla, wq, wkv, gq, gkv, tq1, tq2, tk1, tk2, ctab, stab):
    ntile = NT // MLA_TM
    n_in = p_mla.shape[1]
    vec = lambda r: pl.BlockSpec((r, 128), lambda i: (0, 0))
    head_out = lambda: pl.BlockSpec((1, MLA_H, MLA_TM, 128), lambda i: (_mla_tile_b(i), 0, _mla_tile_r(i), 0))
    shp = jax.ShapeDtypeStruct((NB, MLA_H, LKV, 128), BF16)
    return pl.pallas_call(
        _mla_prep_kernel,
        grid=(ntile,),
        in_specs=[pl.BlockSpec((MLA_TM, n_in), lambda i: (i, 0)),
                  pl.BlockSpec((MLA_QR, MLA_H * 128), lambda i: (0, 0)),
                  pl.BlockSpec((MLA_KVR, MLA_H * 128), lambda i: (0, 0)),
                  pl.BlockSpec((1, MLA_QR), lambda i: (0, 0)),
                  pl.BlockSpec((1, MLA_KVR), lambda i: (0, 0)),
                  vec(1), vec(1), vec(2), vec(1),
                  pl.BlockSpec((MLA_TM, 128), lambda i: (i, 0)),
                  pl.BlockSpec((MLA_TM, 128), lambda i: (i, 0))],
        out_specs=[head_out(), head_out(), head_out()],
        out_shape=[shp, shp, shp],
        compiler_params=_params(("parallel",)),
        name="mla_prep",
    )(p_mla, wq, wkv, gq, gkv, tq1, tq2, tk1, tk2, ctab, stab)


ATT_TQ = 256
ATT_HEADS = 8


def _attend(q, k, v):
    s = lax.dot_general(q, k, (((1,), (1,)), ((), ())), preferred_element_type=F32)
    m = jnp.max(s, axis=-1, keepdims=True)
    e = jnp.exp2(s - m)
    l = jnp.sum(e, axis=-1, keepdims=True)
    return jnp.dot(e.astype(BF16), v, preferred_element_type=F32) / l


def _attention_kernel(q_ref, k_ref, v_ref, o_ref, *, with_ctx):
    qi = pl.program_id(2)
    lane = lax.broadcasted_iota(jnp.int32, (ATT_TQ, 128), 1)

    def run(lo):
        for p in range(ATT_HEADS // 2):
            o0 = _attend(q_ref[0, 2 * p], k_ref[0, 2 * p, lo:, :], v_ref[0, 2 * p, lo:, :])
            o1 = _attend(q_ref[0, 2 * p + 1], k_ref[0, 2 * p + 1, lo:, :], v_ref[0, 2 * p + 1, lo:, :])
            o_ref[:, p * 128:(p + 1) * 128] = jnp.where(lane < MLA_V, pltpu.roll(o0, 64, 1), o1).astype(o_ref.dtype)

    if with_ctx:
        @pl.when(qi < LX // ATT_TQ)
        def _():
            run(0)

        @pl.when(qi == LX // ATT_TQ)
        def _():
            run(LX)
    else:
        run(0)


def attention(qh, kh, vh, with_ctx):
    nq = LX // ATT_TQ + (1 if with_ctx else 0)
    rows = NT if with_ctx else NX

    def out_row(b, qi):
        return jnp.where(qi < LX // ATT_TQ, b * (LX // ATT_TQ) + qi, NX // ATT_TQ + b)

    nh = ATT_HEADS
    kv_spec = lambda: pl.BlockSpec((1, nh, LKV, 128), lambda b, p, qi: (b, p, 0, 0))
    return pl.pallas_call(
        functools.partial(_attention_kernel, with_ctx=with_ctx),
        grid=(NB, MLA_H // nh, nq),
        in_specs=[pl.BlockSpec((1, nh, ATT_TQ, 128), lambda b, p, qi: (b, p, qi, 0)), kv_spec(), kv_spec()],
        out_specs=pl.BlockSpec((ATT_TQ, nh * MLA_V), lambda b, p, qi: (out_row(b, qi), p)),
        out_shape=jax.ShapeDtypeStruct((rows, MLA_H * MLA_V), BF16),
        compiler_params=_params(("parallel", "parallel", "arbitrary")),
        name="mla_attention",
    )(qh, kh, vh)


S5_RX = NX // S5_T
S5_RC = NC // S5_T
S5_R = S5_RX + S5_RC
S5_W = S5_T * 128
S5_ST = 2 * S5_GB * S5_P


def _s5_scan_rows(s_scr, xin_scr, a_re, a_im, reverse):
    half = S5_ST // 2
    cx = LX // S5_T
    cc = LC // S5_T
    zero = jnp.zeros((1, half), F32)

    def sweep(base_of, n, state):
        def body(i, st):
            c = (n - 1 - i) if reverse else i
            new = []
            for b in range(NB):
                r = base_of(b) + c
                xr, xi = st[b]
                xin_scr[pl.ds(r, 1), :] = jnp.concatenate([xr, xi], axis=1)
                srow = s_scr[pl.ds(r, 1), :]
                new.append((a_re * xr - a_im * xi + srow[:, :half],
                            a_re * xi + a_im * xr + srow[:, half:]))
            return tuple(new)
        return lax.fori_loop(0, n, body, state)

    state = tuple((zero, zero) for _ in range(NB))
    state = sweep(lambda b: S5_RX + b * cc, cc, state)
    sweep(lambda b: b * cx, cx, state)


def _s5_kernel(u_ref, tiles_ref, win_ref, wout_ref, at_ref, y_ref, m_scr, s_scr, xin_scr):
    d = pl.program_id(1)
    ucat = jnp.concatenate([u_ref[pl.ds(t, S5_R, stride=S5_T), :] for t in range(S5_T)],
                           axis=1).astype(BF16)
    s_scr[...] = jnp.dot(ucat, win_ref[0, 0], preferred_element_type=F32)
    a_re = at_ref[0, 0][:, :S5_ST // 2]
    a_im = at_ref[0, 0][:, S5_ST // 2:]
    zero_tile = jnp.zeros((128, 128), BF16)

    def build(reverse):
        for tp in range(S5_T):
            for t in range(S5_T):
                lag = (tp - t) if reverse else (t - tp)
                m_scr[tp * 128:(tp + 1) * 128, t * 128:(t + 1) * 128] = (
                    tiles_ref[0, 0, lag] if lag >= 0 else zero_tile)

    @pl.when(d == 0)
    def _():
        build(False)
        _s5_scan_rows(s_scr, xin_scr, a_re, a_im, False)

    @pl.when(d == 1)
    def _():
        build(True)
        _s5_scan_rows(s_scr, xin_scr, a_re, a_im, True)

    ycat = (jnp.dot(ucat, m_scr[...], preferred_element_type=F32)
            + jnp.dot(xin_scr[...].astype(BF16), wout_ref[0, 0], preferred_element_type=F32))

    @pl.when(d == 0)
    def _():
        for t in range(S5_T):
            y_ref[pl.ds(t, S5_R, stride=S5_T), :] = ycat[:, t * 128:(t + 1) * 128]

    @pl.when(d == 1)
    def _():
        for t in range(S5_T):
            y_ref[pl.ds(t, S5_R, stride=S5_T), :] += ycat[:, t * 128:(t + 1) * 128]


def s5_conv(u, tiles, win, wout, at):
    nblk = S5_INNER // 128
    return pl.pallas_call(
        _s5_kernel,
        grid=(nblk, 2),
        in_specs=[pl.BlockSpec((NT, 128), lambda g, d: (0, g)),
                  pl.BlockSpec((1, 1, S5_T, 128, 128), lambda g, d: (d, g, 0, 0, 0)),
                  pl.BlockSpec((1, 1, S5_W, S5_ST), lambda g, d: (d, g, 0, 0)),
                  pl.BlockSpec((1, 1, S5_ST, S5_W), lambda g, d: (d, g, 0, 0)),
                  pl.BlockSpec((1, 1, 1, S5_ST), lambda g, d: (d, g, 0, 0))],
        out_specs=pl.BlockSpec((NT, 128), lambda g, d: (0, g)),
        out_shape=jax.ShapeDtypeStruct((NT, S5_INNER), F32),
        scratch_shapes=[pltpu.VMEM((S5_W, S5_W), BF16), pltpu.VMEM((S5_R, S5_ST), F32),
                        pltpu.VMEM((S5_R, S5_ST), F32)],
        compiler_params=_params(("parallel", "arbitrary")),
        name="s5_conv",
    )(u, tiles, win, wout, at)


def s5_operators(lam_re, lam_im, log_dt, b_re, b_im, c_re, c_im):
    t_len = S5_T
    dt = jnp.exp(log_dt)[..., None]
    mag = jnp.exp(lam_re * dt)
    ab_re = mag * jnp.cos(lam_im * dt)
    ab_im = mag * jnp.sin(lam_im * dt)
    den = lam_re * lam_re + lam_im * lam_im
    nr = ab_re - 1.0
    f_re = ((nr * lam_re + ab_im * lam_im) / den)[..., None]
    f_im = ((ab_im * lam_re - nr * lam_im) / den)[..., None]
    bb_re = f_re * b_re - f_im * b_im
    bb_im = f_re * b_im + f_im * b_re
    pr, pi = [jnp.ones_like(ab_re)], [jnp.zeros_like(ab_re)]
    for _ in range(t_len):
        pr, pi = pr + [pr[-1] * ab_re - pi[-1] * ab_im], pi + [pr[-1] * ab_im + pi[-1] * ab_re]
    pw_re = jnp.stack(pr)
    pw_im = jnp.stack(pi)
    cp_re = c_re[None] * pw_re[:, :, :, None, :] - c_im[None] * pw_im[:, :, :, None, :]
    cp_im = c_re[None] * pw_im[:, :, :, None, :] + c_im[None] * pw_re[:, :, :, None, :]
    kj = (jnp.einsum("jdgip,dgps->jdgis", cp_re[:t_len], bb_re, precision=HIGHEST)
          - jnp.einsum("jdgip,dgps->jdgis", cp_im[:t_len], bb_im, precision=HIGHEST))
    nb = S5_GROUPS // S5_GB

    def block_diag(small, src_of_col, row_group, col_group):
        c = small.shape[-1]
        place = (jnp.arange(c)[:, None] == src_of_col[None, :]).astype(F32)
        keep = row_group[:, None] == col_group[None, :]
        return jnp.where(keep, jnp.einsum("...rc,cn->...rn", small, place), 0.0)

    q128 = jnp.arange(128)
    q1k = jnp.arange(S5_ST)
    kj = kj.reshape(t_len, 2, nb, S5_GB, S5_S, S5_S).transpose(1, 2, 0, 3, 5, 4)
    tiles = block_diag(kj.reshape(2, nb, t_len, 128, S5_S), q128 % S5_S, q128 // S5_S, q128 // S5_S)
    ab_r = pw_re[:t_len, :, :, :, None] * bb_re[None] - pw_im[:t_len, :, :, :, None] * bb_im[None]
    ab_i = pw_re[:t_len, :, :, :, None] * bb_im[None] + pw_im[:t_len, :, :, :, None] * bb_re[None]
    ab = jnp.stack([ab_r, ab_i], axis=-1)
    ab = jnp.stack([ab[::-1, 0], ab[:, 1]], axis=1)
    ab = ab.reshape(t_len, 2, nb, S5_GB, S5_P, S5_S, 2).transpose(1, 2, 0, 3, 5, 6, 4)
    half = S5_ST // 2
    win = block_diag(ab.reshape(2, nb, t_len * 128, 2 * S5_P),
                     (q1k // half) * S5_P + q1k % S5_P,
                     (jnp.arange(t_len * 128) // S5_S) % S5_GB, (q1k % half) // S5_P)
    q_re = jnp.stack([cp_re[1:, 0], cp_re[:0:-1, 1]], axis=1)
    q_im = jnp.stack([cp_im[1:, 0], cp_im[:0:-1, 1]], axis=1)
    qq = jnp.stack([q_re, -q_im], axis=-1)
    qq = qq.reshape(t_len, 2, nb, S5_GB, S5_S, S5_P, 2).transpose(1, 2, 6, 3, 5, 0, 4)
    qw = jnp.arange(t_len * 128)
    wout = block_diag(qq.reshape(2, nb, S5_ST, t_len * S5_S),
                      (qw // 128) * S5_S + qw % S5_S,
                      (q1k % half) // S5_P, (qw % 128) // S5_S)
    at = jnp.stack([pw_re[t_len], pw_im[t_len]], axis=1)
    at = at.reshape(2, 2, nb, S5_GB * S5_P).transpose(0, 2, 1, 3).reshape(2, nb, 1, S5_ST)
    return tiles.astype(BF16), win.astype(BF16), wout.astype(BF16), at


def _s5_finish_kernel(y_ref, u_ref, d_ref, w_ref, o_ref):
    y = y_ref[...] + d_ref[...] * u_ref[...]
    v = jax.nn.gelu(y)
    gate = jnp.dot(v.astype(BF16), w_ref[...], preferred_element_type=F32)
    o_ref[...] = (v * jax.nn.sigmoid(gate)).astype(o_ref.dtype)


def s5_finish(y, u, d_skip, w_glu, rows):
    tm = 512
    blk = lambda: pl.BlockSpec((tm, S5_INNER), lambda i: (i, 0))
    return pl.pallas_call(
        _s5_finish_kernel,
        grid=(rows // tm,),
        in_specs=[blk(), blk(), pl.BlockSpec((1, S5_INNER), lambda i: (0, 0)),
                  pl.BlockSpec((S5_INNER, S5_INNER), lambda i: (0, 0))],
        out_specs=blk(),
        out_shape=jax.ShapeDtypeStruct((rows, S5_INNER), BF16),
        compiler_params=_params(("parallel",)),
        name="s5_finish",
    )(y, u, d_skip.reshape(1, S5_INNER), w_glu)


def _merge_kernel(ssd_ref, mla_ref, s5_ref, g1_ref, g2_ref, g3_ref, w1_ref, w2_ref, w3_ref, o_ref):
    m = (g1_ref[...] * jnp.dot(ssd_ref[...], w1_ref[...], preferred_element_type=F32)
         + g2_ref[...] * jnp.dot(mla_ref[...], w2_ref[...], preferred_element_type=F32)
         + g3_ref[...] * jnp.dot(s5_ref[...], w3_ref[...], preferred_element_type=F32))
    o_ref[...] = m.astype(o_ref.dtype)


def merge(ssd_o, mla_o, s5_o, gates, w1, w2, w3, rows):
    tm, tn = 1024, 512
    nj = D // tn
    act = lambda: pl.BlockSpec((tm, 1024), lambda i, j: (i, 0))
    wsp = lambda: pl.BlockSpec((1024, tn), lambda i, j: (0, j))
    gsp = lambda k: pl.BlockSpec((tm, tn), lambda i, j: (i, k * nj + j))
    return pl.pallas_call(
        _merge_kernel,
        grid=(rows // tm, nj),
        in_specs=[act(), act(), act(), gsp(0), gsp(1), gsp(2), wsp(), wsp(), wsp()],
        out_specs=pl.BlockSpec((tm, tn), lambda i, j: (i, j)),
        out_shape=jax.ShapeDtypeStruct((rows, D), BF16),
        compiler_params=_params(("parallel", "parallel")),
        name="merge",
    )(ssd_o, mla_o, s5_o, gates, gates, gates, w1, w2, w3)


def _outproj_kernel(m_ref, w_ref, x_ref, g_ref, o_ref):
    o_ref[...] = x_ref[...] + g_ref[0] * jnp.dot(m_ref[...], w_ref[...], preferred_element_type=F32)


def outproj(m, w_out, x, modtab, k_gate, rows):
    tm, tn = 1024, 1024
    tpb = LX // tm
    return pl.pallas_call(
        _outproj_kernel,
        grid=(rows // tm, D // tn),
        in_specs=[pl.BlockSpec((tm, D), lambda i, j: (i, 0)),
                  pl.BlockSpec((D, tn), lambda i, j: (0, j)),
                  pl.BlockSpec((tm, tn), lambda i, j: (i, j)),
                  pl.BlockSpec((1, 1, tn), lambda i, j: (_mod_row(i, tpb) * 6 + k_gate, 0, j))],
        out_specs=pl.BlockSpec((tm, tn), lambda i, j: (i, j)),
        out_shape=jax.ShapeDtypeStruct((rows, D), F32),
        compiler_params=_params(("parallel", "parallel")),
        name="outproj",
    )(m, w_out, x, modtab)


def _router_kernel(x_ref, g_ref, sc_ref, sh_ref, w_ref, h_ref, o_ref):
    x = x_ref[...]
    ms = jnp.mean(x * x, axis=-1, keepdims=True)
    h = x * lax.rsqrt(ms + EPS) * g_ref[...] * (1.0 + sc_ref[0]) + sh_ref[0]
    h_hi = h.astype(BF16)
    h_ref[...] = h_hi
    h_lo = (h - h_hi.astype(F32)).astype(BF16)
    w = w_ref[...]
    w_hi = w.astype(BF16)
    w_lo = (w - w_hi.astype(F32)).astype(BF16)
    nt = lambda a, b: lax.dot_general(a, b, (((1,), (1,)), ((), ())), preferred_element_type=F32)
    logits = nt(w_hi, h_hi) + nt(w_lo, h_hi) + nt(w_hi, h_lo)
    m = jnp.max(logits, axis=0, keepdims=True)
    e = jnp.exp(logits - m)
    o_ref[...] = e / jnp.sum(e, axis=0, keepdims=True)


def router(x, gain, modtab, w_router_t, rows):
    tm = 256
    tpb = LX // tm
    return pl.pallas_call(
        _router_kernel,
        grid=(rows // tm,),
        in_specs=[pl.BlockSpec((tm, D), lambda i: (i, 0)),
                  pl.BlockSpec((1, D), lambda i: (0, 0)),
                  pl.BlockSpec((1, 1, D), lambda i: (_mod_row(i, tpb) * 6 + 4, 0, 0)),
                  pl.BlockSpec((1, 1, D), lambda i: (_mod_row(i, tpb) * 6 + 3, 0, 0)),
                  pl.BlockSpec((N_EXP, D), lambda i: (0, 0))],
        out_specs=[pl.BlockSpec((tm, D), lambda i: (i, 0)),
                   pl.BlockSpec((N_EXP, tm), lambda i: (0, i))],
        out_shape=[jax.ShapeDtypeStruct((rows, D), BF16), jax.ShapeDtypeStruct((N_EXP, rows), F32)],
        compiler_params=_params(("parallel",)),
        name="moe_router",
    )(x, gain.reshape(1, D), modtab, modtab, w_router_t)


def _prefix_count(mask_f, n):
    row = lax.broadcasted_iota(jnp.int32, (128, 128), 0)
    col = lax.broadcasted_iota(jnp.int32, (128, 128), 1)
    upper = (row < col).astype(BF16)
    offset = jnp.zeros((N_EXP, 1), F32)
    parts = []
    for j in range(n // 128):
        blk = mask_f[:, j * 128:(j + 1) * 128]
        parts.append(jnp.dot(blk.astype(BF16), upper, preferred_element_type=F32) + offset)
        offset = offset + jnp.sum(blk, axis=1, keepdims=True)
    return jnp.concatenate(parts, axis=1)


def _select_kernel(aff_ref, g_ref, pos_ref, *, n, cap):
    aff = aff_ref[...]
    capf = float(cap)

    def count_ge(t):
        return jnp.sum((aff >= t).astype(F32), axis=1, keepdims=True)

    def bisect(_, c):
        lo, hi = c
        mid = 0.5 * (lo + hi)
        ok = count_ge(mid) >= capf
        return jnp.where(ok, mid, lo), jnp.where(ok, hi, mid)

    lo, hi = lax.fori_loop(0, 40, bisect, (jnp.zeros((N_EXP, 1), F32), jnp.full((N_EXP, 1), 2.0, F32)))

    def unfinished(c):
        return jnp.min(c[2]) < 0.5

    def step(c):
        cur, thr, done = c
        cand = jnp.max(jnp.where(aff < cur, aff, -1.0), axis=1, keepdims=True)
        ok = (count_ge(cand) >= capf).astype(F32)
        thr = jnp.where(done > 0.5, thr, cand)
        done = jnp.maximum(done, ok)
        return jnp.where(done > 0.5, cur, cand), thr, done

    _, thr, _ = lax.while_loop(unfinished, step, (hi, lo, jnp.zeros((N_EXP, 1), F32)))
    gt = aff > thr
    eq = (aff == thr).astype(F32)
    need = cap - jnp.sum(gt.astype(F32), axis=1, keepdims=True)
    sel = jnp.logical_or(gt, jnp.logical_and(eq > 0, _prefix_count(eq, n) < need))
    sel_f = sel.astype(F32)
    g_ref[...] = jnp.where(sel, aff, 0.0)
    pos_ref[...] = jnp.where(sel, _prefix_count(sel_f, n), -1.0)


def select(aff_t, n, cap, row0):
    b0 = row0 // n
    spec = lambda: pl.BlockSpec((N_EXP, n), lambda b: (0, b))
    shp = jax.ShapeDtypeStruct((N_EXP, NB * n), F32)
    return pl.pallas_call(
        functools.partial(_select_kernel, n=n, cap=cap),
        grid=(NB,),
        in_specs=[pl.BlockSpec((N_EXP, n), lambda b: (0, b0 + b))],
        out_specs=[spec(), spec()],
        out_shape=[shp, shp],
        compiler_params=_params(("parallel",)),
        name="moe_select",
    )(aff_t)


def _gather_kernel(h_ref, pos_ref, g_ref, xs_ref, gs_ref, *, n, cap):
    pos = pos_ref[0, 0]
    slot = lax.broadcasted_iota(jnp.int32, (cap, n), 0).astype(F32)
    onehot = slot == pos
    xs_ref[0] = jnp.dot(onehot.astype(BF16), h_ref[...], preferred_element_type=F32).astype(BF16)
    gsel = jnp.sum(jnp.where(onehot, g_ref[0, 0], 0.0), axis=1, keepdims=True)
    gs_ref[0] = jnp.broadcast_to(gsel, (cap, 128))


def gather(h2, pos_t, g_t, n, cap, row0):
    b0 = row0 // n
    tab = lambda: pl.BlockSpec((1, 1, 1, n), lambda b, e: (e, b, 0, 0))
    return pl.pallas_call(
        functools.partial(_gather_kernel, n=n, cap=cap),
        grid=(NB, N_EXP),
        in_specs=[pl.BlockSpec((n, D), lambda b, e: (b0 + b, 0)), tab(), tab()],
        out_specs=[pl.BlockSpec((1, cap, D), lambda b, e: (e, b, 0)),
                   pl.BlockSpec((1, cap, 128), lambda b, e: (e, b, 0))],
        out_shape=[jax.ShapeDtypeStruct((N_EXP, NB * cap, D), BF16),
                   jax.ShapeDtypeStruct((N_EXP, NB * cap, 128), F32)],
        compiler_params=_params(("parallel", "arbitrary")),
        name="moe_gather",
    )(h2, pos_t, g_t)


EXP_TF = 512
EXP_TN = 512
EXP_NF = FF // EXP_TF
EXP_NN = D // EXP_TN


def _expert_kernel(*refs, with_ctx):
    if with_ctx:
        xs_ref, xc_ref, gs_ref, gc_ref, wg_ref, wu_ref, wd_ref, ys_ref, yc_ref, hid_ref = refs
    else:
        xs_ref, gs_ref, wg_ref, wu_ref, wd_ref, ys_ref, hid_ref = refs
    f = pl.program_id(1)

    @pl.when(f < EXP_NF)
    def _():
        xs = jnp.concatenate([xs_ref[0], xc_ref[0]], axis=0) if with_ctx else xs_ref[0]
        hg = jnp.dot(xs, wg_ref[0].astype(BF16), preferred_element_type=F32)
        hu = jnp.dot(xs, wu_ref[0].astype(BF16), preferred_element_type=F32)
        hid_ref[f] = (hg * jax.nn.sigmoid(hg) * hu).astype(BF16)

    @pl.when(f >= EXP_NF)
    def _():
        acc = None
        for j in range(EXP_NF):
            part = jnp.dot(hid_ref[j], wd_ref[0, j * EXP_TF:(j + 1) * EXP_TF, :].astype(BF16),
                           preferred_element_type=F32)
            acc = part if acc is None else acc + part
        rx = NB * CAP_X
        ys_ref[0] = (acc[0:rx, :] * gs_ref[0][:, 0:1]).astype(BF16)
        if with_ctx:
            yc_ref[0] = (acc[rx:, :] * gc_ref[0][:, 0:1]).astype(BF16)


def experts(xs, gs, w_gate, w_up, w_down, layer, xc=None, gc=None):
    with_ctx = xc is not None
    rx, rc = NB * CAP_X, NB * CAP_C
    e0 = layer * N_EXP
    up_tile = lambda f: jnp.minimum(f, EXP_NF - 1)
    down_tile = lambda f: jnp.maximum(f - EXP_NF, 0)
    tok = lambda r, w: pl.BlockSpec((1, r, w), lambda e, f: (e, 0, 0))
    out_tile = lambda r: pl.BlockSpec((1, r, EXP_TN), lambda e, f: (e, 0, down_tile(f)))
    in_specs = [tok(rx, D)] + ([tok(rc, D)] if with_ctx else []) + [tok(rx, 128)] + ([tok(rc, 128)] if with_ctx else [])
    in_specs += [pl.BlockSpec((1, D, EXP_TF), lambda e, f: (e0 + e, 0, up_tile(f))),
                 pl.BlockSpec((1, D, EXP_TF), lambda e, f: (e0 + e, 0, up_tile(f))),
                 pl.BlockSpec((1, FF, EXP_TN), lambda e, f: (e0 + e, 0, down_tile(f)))]
    out_specs = [out_tile(rx)] + ([out_tile(rc)] if with_ctx else [])
    out_shape = [jax.ShapeDtypeStruct((N_EXP, rx, D), BF16)] + (
        [jax.ShapeDtypeStruct((N_EXP, rc, D), BF16)] if with_ctx else [])
    args = [xs] + ([xc] if with_ctx else []) + [gs] + ([gc] if with_ctx else []) + [w_gate, w_up, w_down]
    return pl.pallas_call(
        functools.partial(_expert_kernel, with_ctx=with_ctx),
        grid=(N_EXP, EXP_NF + EXP_NN),
        in_specs=in_specs,
        out_specs=out_specs,
        out_shape=out_shape,
        scratch_shapes=[pltpu.VMEM((EXP_NF, rx + (rc if with_ctx else 0), EXP_TF), BF16)],
        compiler_params=_params(("parallel", "arbitrary")),
        name="moe_experts",
    )(*args)


def _combine_kernel(*refs, cap, aliased):
    pos_ref, ys_ref, x_ref, g_ref, o_ref = refs[1:] if aliased else refs
    pos = pos_ref[...]
    tm = pos.shape[0]
    cap_pad = max(cap, 128)
    slot = lax.broadcasted_iota(jnp.int32, (tm, cap_pad), 1).astype(F32)
    acc = None
    for e in range(N_EXP):
        onehot = (pos[:, e:e + 1] == slot).astype(BF16)
        ys = ys_ref[e]
        if cap_pad > cap:
            ys = jnp.concatenate([ys, jnp.zeros((cap_pad - cap, D), BF16)], axis=0)
        part = jnp.dot(onehot, ys, preferred_element_type=F32)
        acc = part if acc is None else acc + part
    o_ref[...] = x_ref[...] + g_ref[0] * acc


def combine(pos_col, ys, x, modtab, k_gate, n, cap, row0, mod_row_of_batch, out_rows, prev=None):
    tm = 256
    tpb = n // tm
    r0 = row0 // tm
    aliased = prev is not None
    in_specs = [pl.BlockSpec((tm, N_EXP), lambda b, t: (b * tpb + t, 0)),
                pl.BlockSpec((N_EXP, cap, D), lambda b, t: (0, b, 0)),
                pl.BlockSpec((tm, D), lambda b, t: (r0 + b * tpb + t, 0)),
                pl.BlockSpec((1, 1, D), lambda b, t: (mod_row_of_batch(b) * 6 + k_gate, 0, 0))]
    args = [pos_col, ys, x, modtab]
    if aliased:
        in_specs = [pl.BlockSpec(memory_space=pl.ANY)] + in_specs
        args = [prev] + args
    return pl.pallas_call(
        functools.partial(_combine_kernel, cap=cap, aliased=aliased),
        grid=(NB, tpb),
        in_specs=in_specs,
        out_specs=pl.BlockSpec((tm, D), lambda b, t: (r0 + b * tpb + t, 0)),
        out_shape=jax.ShapeDtypeStruct((out_rows, D), F32),
        input_output_aliases={0: 0} if aliased else {},
        compiler_params=_params(("parallel", "parallel")),
        name="moe_combine",
    )(*args)


def moe_layer(xa, gain, modtab, w_router_t, w_gate, w_up, w_down, layer, with_ctx):
    rows = NT if with_ctx else NX
    h2, aff = router(xa, gain, modtab, w_router_t, rows)

    def route(n, cap, row0):
        g_t, pos_t = select(aff, n, cap, row0)
        xs, gs = gather(h2, pos_t.reshape(N_EXP, NB, 1, n), g_t.reshape(N_EXP, NB, 1, n), n, cap, row0)
        return pos_t.T, xs, gs

    pos_x, xs_x, gs_x = route(LX, CAP_X, 0)
    if with_ctx:
        pos_c, xs_c, gs_c = route(LC, CAP_C, NX)
        ys_x, ys_c = experts(xs_x, gs_x, w_gate, w_up, w_down, layer, xs_c, gs_c)
    else:
        (ys_x,) = experts(xs_x, gs_x, w_gate, w_up, w_down, layer)
    out = combine(pos_x, ys_x, xa, modtab, 5, LX, CAP_X, 0, lambda b: b, rows)
    if with_ctx:
        out = combine(pos_c, ys_c, xa, modtab, 5, LC, CAP_C, NX, lambda b: NB, rows, prev=out)
    return out


def _rope_tables():
    n_freq = MLA_ROPE // 4
    inv = ROPE_BASE ** (-jnp.arange(n_freq, dtype=F32) / n_freq)
    rows = jnp.repeat(jnp.arange(LX // GRID_W, dtype=F32), GRID_W)
    cols = jnp.tile(jnp.arange(GRID_W, dtype=F32), LX // GRID_W)
    ang = jnp.concatenate([rows[:, None] * inv, cols[:, None] * inv], axis=-1)
    cos, sin = jnp.cos(ang), jnp.sin(ang)
    one = jnp.ones((LX, MLA_NOPE), F32)
    zero = jnp.zeros((LX, 32), F32)
    ct_x = jnp.concatenate([one, cos, cos, zero], axis=1)
    st_x = jnp.concatenate([0 * one, -sin, sin, zero], axis=1)
    ct_c = jnp.concatenate([jnp.ones((NC, 96), F32), jnp.zeros((NC, 32), F32)], axis=1)
    ctab = jnp.concatenate([jnp.tile(ct_x, (NB, 1)), ct_c], axis=0)
    stab = jnp.concatenate([jnp.tile(st_x, (NB, 1)), jnp.zeros((NC, 128), F32)], axis=0)
    return ctab, stab


_PERM = list(range(0, MLA_ROPE, 2)) + list(range(1, MLA_ROPE, 2))
_PERM_SW = list(range(1, MLA_ROPE, 2)) + list(range(0, MLA_ROPE, 2))


def _lane_table(first64, rope32):
    return jnp.concatenate([first64, rope32, jnp.zeros((32,), F32)]).reshape(1, 128)


def _layer_weights(i, w_in, ssd_a_log, ssd_dt_bias, mla_w_q_b, mla_w_kv_b, mla_q_gain, mla_k_gain):
    w = w_in[i]
    wdt = w[:, SSD_INNER + SSD_CONV_CH:O1].reshape(D, SSD_G, SSD_HPG)
    wdt_col = jnp.pad(wdt, ((0, 0), (0, 0), (0, 128 - SSD_HPG))).reshape(D, SSD_G * 128)
    w_ssd = jnp.concatenate([w[:, SSD_INNER:SSD_INNER + SSD_CONV_CH], w[:, :SSD_INNER], wdt_col],
                            axis=1).astype(BF16)
    wdt_row = jnp.pad(jnp.transpose(wdt, (1, 2, 0)), ((0, 0), (0, 8 - SSD_HPG), (0, 0))).reshape(SSD_G * 8, D)
    kpe = w[:, O1 + MLA_QR + MLA_KVR:O2]
    kpe128 = jnp.concatenate([jnp.zeros((D, MLA_NOPE), F32), kpe[:, _PERM], kpe[:, _PERM_SW]], axis=1)
    w_mla = jnp.concatenate([w[:, O1:O1 + MLA_QR + MLA_KVR], kpe128], axis=1).astype(BF16)
    w_s5 = w[:, O2:O3].astype(BF16)
    w_gate = w[:, O3:].astype(BF16)
    par = jnp.stack([ssd_a_log[i, 0], ssd_a_log[i, 1], ssd_dt_bias[i, 0], ssd_dt_bias[i, 1]])
    par = par.reshape(4, SSD_G, SSD_HPG).transpose(1, 0, 2)
    pcol = jnp.pad(par, ((0, 0), (0, 4), (0, 128 - SSD_HPG)))
    prow = jnp.broadcast_to(jnp.pad(par, ((0, 0), (0, 0), (0, 8 - SSD_HPG)))[..., None], (SSD_G, 4, 8, 128))
    wq = mla_w_q_b[i].reshape(MLA_QR, MLA_H, MLA_NOPE + MLA_ROPE)
    wq = jnp.concatenate([wq[..., :MLA_NOPE], wq[..., MLA_NOPE:][..., _PERM], wq[..., MLA_NOPE:][..., _PERM_SW]],
                         axis=-1).reshape(MLA_QR, MLA_H * 128).astype(BF16)
    wkv = mla_w_kv_b[i].astype(BF16)
    qg, kg = mla_q_gain[i], mla_k_gain[i]
    tq1 = _lane_table(qg[:MLA_NOPE], qg[MLA_NOPE:][jnp.array(_PERM)])
    tq2 = _lane_table(jnp.zeros((MLA_NOPE,), F32), qg[MLA_NOPE:][jnp.array(_PERM_SW)])
    tk1 = jnp.concatenate([_lane_table(kg[:MLA_NOPE], jnp.zeros((MLA_ROPE,), F32)),
                           _lane_table(jnp.zeros((MLA_NOPE,), F32), kg[MLA_NOPE:][jnp.array(_PERM)])], axis=0)
    tk2 = _lane_table(jnp.zeros((MLA_NOPE,), F32), kg[MLA_NOPE:][jnp.array(_PERM_SW)])
    return dict(w_ssd=w_ssd, wdt_row=wdt_row.astype(BF16), w_mla=w_mla, w_s5=w_s5, w_gate=w_gate,
                pcol=pcol, prow=prow, wq=wq, wkv=wkv, tq1=tq1, tq2=tq2, tk1=tk1, tk2=tk2)


def kernel(x, c, ctx, c_ctx, norm1_gain, norm2_gain, w_mod, b_mod, w_in, ssd_conv_w, ssd_conv_b, ssd_a_log, ssd_dt_bias, ssd_d, ssd_norm_gain, mla_q_a_gain, mla_kv_a_gain, mla_w_q_b, mla_w_kv_b, mla_q_gain, mla_k_gain, s5_lam_re, s5_lam_im, s5_log_dt, s5_b_re, s5_b_im, s5_c_re, s5_c_im, s5_d, s5_w_glu, w_branch_ssd, w_branch_mla, w_branch_s5, w_out, moe_router, moe_w_gate, moe_w_up, moe_w_down):
    ctab, stab = _rope_tables()
    cvec = jnp.concatenate([c, c_ctx[None], jnp.zeros((8 - NB - 1, D), F32)], axis=0)
    mod = modulation(cvec, w_mod, b_mod)
    xa = jnp.concatenate([x.reshape(NX, D), ctx.reshape(NC, D)], axis=0)
    for i in range(DEPTH):
        need_ctx = i < DEPTH - 1
        rows = NT if need_ctx else NX
        modtab = mod[i].reshape(8 * 6, 1, D)
        lw = _layer_weights(i, w_in, ssd_a_log, ssd_dt_bias, mla_w_q_b, mla_w_kv_b, mla_q_gain, mla_k_gain)
        h = normmod(xa, norm1_gain[i], modtab, 0, 1, NT)
        p_ssd = matmul(h, lw["w_ssd"], 1024, 512, name="inproj_ssd")
        dt_rows = matmul_nt(lw["wdt_row"], h, 1024)
        p_mla = matmul(h, lw["w_mla"], 1024, 896, name="inproj_mla")
        u_s5 = matmul(h, lw["w_s5"], 1024, 512, name="inproj_s5")
        gates = matmul(h, lw["w_gate"], 1024, 1024, act="sigmoid", out_dtype=BF16, name="inproj_gate", rows=rows)
        xbc = ssd_conv(p_ssd, ssd_conv_w[i], ssd_conv_b[i])
        yf, yr = ssd_scan(xbc, p_ssd, dt_rows, lw["pcol"], lw["prow"])
        dsum = jnp.repeat(ssd_d[i, 0] + ssd_d[i, 1], SSD_P).reshape(1, SSD_INNER)
        ssd_o = ssd_finish(yf, yr, xbc, p_ssd, dsum, ssd_norm_gain[i], rows)
        qh, kh, vh = mla_prep(p_mla, lw["wq"], lw["wkv"], mla_q_a_gain[i].reshape(1, MLA_QR),
                              mla_kv_a_gain[i].reshape(1, MLA_KVR), lw["tq1"], lw["tq2"], lw["tk1"], lw["tk2"],
                              ctab, stab)
        mla_o = attention(qh, kh, vh, need_ctx)
        ops = s5_operators(s5_lam_re[i], s5_lam_im[i], s5_log_dt[i], s5_b_re[i], s5_b_im[i], s5_c_re[i], s5_c_im[i])
        y_s5 = s5_conv(u_s5, *ops)
        s5_o = s5_finish(y_s5, u_s5, s5_d[i], s5_w_glu[i].astype(BF16), rows)
        m = merge(ssd_o, mla_o, s5_o, gates, w_branch_ssd[i].astype(BF16), w_branch_mla[i].astype(BF16),
                  w_branch_s5[i].astype(BF16), rows)
        xa = outproj(m, w_out[i].astype(BF16), xa, modtab, 2, rows)
        xa = moe_layer(xa, norm2_gain[i], modtab, moe_router[i].T, moe_w_gate.reshape(DEPTH * N_EXP, D, FF),
                       moe_w_up.reshape(DEPTH * N_EXP, D, FF), moe_w_down.reshape(DEPTH * N_EXP, FF, D), i, need_ctx)
    return xa[:NX].reshape(NB, LX, D)
```

```python
import functools
import math

import jax
import jax.numpy as jnp
from jax import lax
from jax.experimental import pallas as pl
from jax.experimental.pallas import tpu as pltpu

F32 = jnp.float32
BF16 = jnp.bfloat16
HIGHEST = lax.Precision.HIGHEST

D = 2048
NB = 4
LX = 2048
LC = 256
NX = NB * LX
NC = NB * LC
NT = NX + NC
DEPTH = 2
EPS = 1e-6
GRID_W = 64

SSD_HEADS = 16
SSD_P = 64
SSD_INNER = 1024
SSD_G = 4
SSD_HPG = 4
SSD_N = 128
SSD_Q = 128
SSD_BC = SSD_G * SSD_N
SSD_CONV_CH = SSD_INNER + 2 * SSD_BC
SSD_IN = SSD_INNER + SSD_CONV_CH + SSD_HEADS

MLA_H = 16
MLA_QR = 512
MLA_KVR = 256
MLA_NOPE = 64
MLA_ROPE = 32
MLA_V = 64
MLA_IN = MLA_QR + MLA_KVR + MLA_ROPE
ROPE_BASE = 10000.0

S5_INNER = 1024
S5_S = 16
S5_GROUPS = 64
S5_P = 64
S5_T = 8
S5_GB = 8

N_EXP = 16
FF = 2048
CAP_X = 2 * LX // N_EXP
CAP_C = 2 * LC // N_EXP

O1 = SSD_IN
O2 = O1 + MLA_IN
O3 = O2 + S5_INNER

VMEM_LIMIT = 56 * 1024 * 1024


def _params(sem):
    return pltpu.CompilerParams(dimension_semantics=sem, vmem_limit_bytes=VMEM_LIMIT)


def _mod_row(i, tiles_per_batch):
    return jnp.minimum(i // tiles_per_batch, NB)


def _modulation_kernel(c_ref, w_ref, b_ref, o_ref):
    c = c_ref[...]
    a = (c * jax.nn.sigmoid(c)).astype(BF16)
    o_ref[0] = jnp.dot(a, w_ref[0].astype(BF16), preferred_element_type=F32) + b_ref[0]


def modulation(cvec, w_mod, b_mod):
    tn = 1536
    n = w_mod.shape[-1]
    return pl.pallas_call(
        _modulation_kernel,
        grid=(DEPTH, n // tn),
        in_specs=[pl.BlockSpec((8, D), lambda l, j: (0, 0)),
                  pl.BlockSpec((1, D, tn), lambda l, j: (l, 0, j)),
                  pl.BlockSpec((1, 1, tn), lambda l, j: (l, 0, j))],
        out_specs=pl.BlockSpec((1, 8, tn), lambda l, j: (l, 0, j)),
        out_shape=jax.ShapeDtypeStruct((DEPTH, 8, n), F32),
        compiler_params=_params(("parallel", "parallel")),
        name="modulation",
    )(cvec, w_mod, b_mod.reshape(DEPTH, 1, n))


def _row_specs(tm, tn, col, ctx_given):
    if not ctx_given:
        return [pl.BlockSpec((tm, tn), lambda i, *j: (i, col(*j)))]
    nxt = NX // tm
    return [pl.BlockSpec((tm, tn), lambda i, *j: (jnp.minimum(i, nxt - 1), col(*j))),
            pl.BlockSpec((tm, tn), lambda i, *j: (jnp.maximum(i - nxt, 0), col(*j)))]


def _pick_rows(refs, tm):
    if len(refs) == 1:
        return refs[0][...]
    return jnp.where(pl.program_id(0) < NX // tm, refs[0][...], refs[1][...])


def _normmod_kernel(*refs, tm):
    *x_refs, g_ref, sc_ref, sh_ref, o_ref = refs
    x = _pick_rows(x_refs, tm)
    ms = jnp.mean(x * x, axis=-1, keepdims=True)
    y = x * lax.rsqrt(ms + EPS) * g_ref[...]
    o_ref[...] = (y * (1.0 + sc_ref[0]) + sh_ref[0]).astype(o_ref.dtype)


def normmod(x, gain, modtab, k_shift, k_scale, rows, ctx=None):
    tm = 256
    tpb = LX // tm
    xs = [x] if ctx is None else [x, ctx]
    return pl.pallas_call(
        functools.partial(_normmod_kernel, tm=tm),
        grid=(rows // tm,),
        in_specs=_row_specs(tm, D, lambda: 0, ctx is not None) + [
            pl.BlockSpec((1, D), lambda i: (0, 0)),
            pl.BlockSpec((1, 1, D), lambda i: (_mod_row(i, tpb) * 6 + k_scale, 0, 0)),
            pl.BlockSpec((1, 1, D), lambda i: (_mod_row(i, tpb) * 6 + k_shift, 0, 0))],
        out_specs=pl.BlockSpec((tm, D), lambda i: (i, 0)),
        out_shape=jax.ShapeDtypeStruct((rows, D), BF16),
        compiler_params=_params(("parallel",)),
        name="normmod",
    )(*xs, gain.reshape(1, D), modtab, modtab)


def _mm_kernel(a_ref, w_ref, o_ref, *, act):
    acc = jnp.dot(a_ref[...], w_ref[...], preferred_element_type=F32)
    if act == "sigmoid":
        acc = jax.nn.sigmoid(acc)
    o_ref[...] = acc.astype(o_ref.dtype)


def matmul(a, w, tm, tn, act=None, out_dtype=F32, name="matmul", rows=None):
    m, k = a.shape
    m = rows or m
    n = w.shape[1]
    return pl.pallas_call(
        functools.partial(_mm_kernel, act=act),
        grid=(m // tm, n // tn),
        in_specs=[pl.BlockSpec((tm, k), lambda i, j: (i, 0)),
                  pl.BlockSpec((k, tn), lambda i, j: (0, j))],
        out_specs=pl.BlockSpec((tm, tn), lambda i, j: (i, j)),
        out_shape=jax.ShapeDtypeStruct((m, n), out_dtype),
        compiler_params=_params(("parallel", "parallel")),
        name=name,
    )(a, w)


def _mm_nt_kernel(w_ref, a_ref, o_ref):
    o_ref[...] = lax.dot_general(w_ref[...], a_ref[...], (((1,), (1,)), ((), ())),
                                 preferred_element_type=F32)


def matmul_nt(w_rows, a, tm):
    r, k = w_rows.shape
    m = a.shape[0]
    return pl.pallas_call(
        _mm_nt_kernel,
        grid=(m // tm,),
        in_specs=[pl.BlockSpec((r, k), lambda i: (0, 0)),
                  pl.BlockSpec((tm, k), lambda i: (i, 0))],
        out_specs=pl.BlockSpec((r, tm), lambda i: (0, i)),
        out_shape=jax.ShapeDtypeStruct((r, m), F32),
        compiler_params=_params(("parallel",)),
        name="matmul_nt",
    )(w_rows, a)


CONV_TM = 256
CONV_TC = 2048
CONV_HALO = 8


def _conv_kernel(prev_ref, cur_ref, next_ref, w_ref, b_ref, o_ref, ext_ref):
    i = pl.program_id(0)
    blocks_per_seq = LX // CONV_TM
    is_ctx = i >= NX // CONV_TM
    first = jnp.logical_or(is_ctx, i % blocks_per_seq == 0)
    last = jnp.logical_or(is_ctx, i % blocks_per_seq == blocks_per_seq - 1)
    zeros = jnp.zeros((CONV_HALO, CONV_TC), F32)
    ext_ref[0:CONV_HALO, :] = jnp.where(first, zeros, prev_ref[...])
    ext_ref[CONV_HALO:CONV_HALO + CONV_TM, :] = cur_ref[...]
    ext_ref[CONV_HALO + CONV_TM:, :] = jnp.where(last, zeros, next_ref[...])
    acc = b_ref[...] + jnp.zeros((CONV_TM, CONV_TC), F32)
    for k in range(5):
        acc = acc + ext_ref[pl.ds(CONV_HALO + k - 2, CONV_TM), :] * w_ref[k:k + 1, :]
    o_ref[...] = acc * jax.nn.sigmoid(acc)


def ssd_conv(p_ssd, conv_w, conv_b):
    nblk = NT // CONV_TM
    hb = CONV_TM // CONV_HALO
    c0 = 0
    w8 = jnp.zeros((8, SSD_CONV_CH), F32).at[:5].set(conv_w)
    return pl.pallas_call(
        _conv_kernel,
        grid=(nblk, SSD_CONV_CH // CONV_TC),
        in_specs=[pl.BlockSpec((CONV_HALO, CONV_TC), lambda i, j: (jnp.maximum(i * hb - 1, 0), c0 + j)),
                  pl.BlockSpec((CONV_TM, CONV_TC), lambda i, j: (i, c0 + j)),
                  pl.BlockSpec((CONV_HALO, CONV_TC),
                               lambda i, j: (jnp.minimum((i + 1) * hb, NT // CONV_HALO - 1), c0 + j)),
                  pl.BlockSpec((8, CONV_TC), lambda i, j: (0, j)),
                  pl.BlockSpec((1, CONV_TC), lambda i, j: (0, j))],
        out_specs=pl.BlockSpec((CONV_TM, CONV_TC), lambda i, j: (i, j)),
        out_shape=jax.ShapeDtypeStruct((NT, SSD_CONV_CH), F32),
        scratch_shapes=[pltpu.VMEM((CONV_TM + 2 * CONV_HALO, CONV_TC), F32)],
        compiler_params=_params(("parallel", "parallel")),
        name="ssd_conv",
    )(p_ssd, p_ssd, p_ssd, w8, conv_b.reshape(1, SSD_CONV_CH))


NCH_C = LC // SSD_Q
NCH_X = LX // SSD_Q
SSD_STEPS = NCH_C + NCH_X


def _softplus(x):
    return jnp.maximum(x, 0.0) + jnp.log1p(jnp.exp(-jnp.abs(x)))


def _split3(a):
    p1 = a.astype(BF16).astype(F32)
    r = a - p1
    p2 = r.astype(BF16).astype(F32)
    p3 = (r - p2).astype(BF16).astype(F32)
    return p1, p2, p3


def _ssd_one_direction(d, g, xs_ref, b_ref, c_ref, dtc_ref, dtr_ref, pcol, prow, st_ref, y_ref, row, col):
    q = SSD_Q
    gw = SSD_HPG * SSD_P
    xs = xs_ref[:, g * gw:(g + 1) * gw]
    bm = b_ref[:, g * SSD_N:(g + 1) * SSD_N].astype(BF16)
    cm = c_ref[:, g * SSD_N:(g + 1) * SSD_N].astype(BF16)
    tri_l = (col <= row).astype(F32)
    tri_u = (col >= row).astype(F32)
    t_col = tri_l if d == 0 else tri_u
    t_row = tri_u if d == 0 else tri_l
    mask = (col <= row) if d == 0 else (col >= row)
    dt_c = _softplus(dtc_ref[:, g * 128:(g + 1) * 128] + pcol[2 + d:3 + d, :])
    a_c = dt_c * (-jnp.exp(pcol[d:d + 1, :]))
    t_col16 = t_col.astype(BF16)
    c1, c2, c3 = (jnp.dot(t_col16, p.astype(BF16), preferred_element_type=F32) for p in _split3(a_c))
    cs_c = c1 + c2 + c3
    sub = lax.broadcasted_iota(jnp.int32, (8, q), 0)
    dt_r = _softplus(dtr_ref[g * 8:(g + 1) * 8, :] + prow[2 + d][:, 0:1])
    a_r = jnp.where(sub < SSD_HPG, dt_r * (-jnp.exp(prow[d][:, 0:1])), 0.0)
    stacked = jnp.concatenate(_split3(a_r), axis=0).astype(BF16)
    sums_r = jnp.dot(stacked, t_row.astype(BF16), preferred_element_type=F32)
    cs_r = sums_r[0:8] + sums_r[8:16] + sums_r[16:24]
    edge = q - 1 if d == 0 else 0
    tot_c = cs_c[edge:edge + 1, :]
    tot_r = cs_r[:, edge:edge + 1]

    def lanes(v):
        return jnp.concatenate([jnp.broadcast_to(v[:, h:h + 1], (v.shape[0], SSD_P))
                                for h in range(SSD_HPG)], axis=1)

    dt_b = lanes(dt_c)
    cs_b = lanes(cs_c)
    tot_b = lanes(tot_c)
    xdt = xs * dt_b
    scores = lax.dot_general(cm, bm, (((1,), (1,)), ((), ())), preferred_element_type=F32)
    st = st_ref[d, g]
    y_off = lax.dot_general(cm, st.astype(BF16), (((1,), (1,)), ((), ())),
                            preferred_element_type=F32) * jnp.exp(cs_b)
    xdt16 = xdt.astype(BF16)
    ys = []
    for h in range(SSD_HPG):
        diff = cs_c[:, h:h + 1] - cs_r[h:h + 1, :]
        decay = jnp.exp(jnp.where(mask, diff, -jnp.inf))
        ys.append(jnp.dot((scores * decay).astype(BF16), xdt16[:, h * SSD_P:(h + 1) * SSD_P],
                          preferred_element_type=F32))
    y_ref[:, g * gw:(g + 1) * gw] = jnp.concatenate(ys, axis=1) + y_off
    xw_t = (xdt * jnp.exp(tot_b - cs_b)).T.astype(BF16)
    s_new = jnp.dot(xw_t, bm, preferred_element_type=F32)
    dec = jnp.concatenate([jnp.broadcast_to(jnp.exp(tot_r[h:h + 1, :]), (SSD_P, SSD_N))
                           for h in range(SSD_HPG)], axis=0)
    st_ref[d, g] = st * dec + s_new


def _ssd_kernel(xf, bf, cf, dtcf, dtrf, xr, br, cr, dtcr, dtrr, pcol_ref, prow_ref, yf_ref, yr_ref, st_ref):
    k = pl.program_id(1)

    @pl.when(k == 0)
    def _():
        st_ref[...] = jnp.zeros_like(st_ref)

    row = lax.broadcasted_iota(jnp.int32, (SSD_Q, SSD_Q), 0)
    col = lax.broadcasted_iota(jnp.int32, (SSD_Q, SSD_Q), 1)
    for g in range(SSD_G):
        pcol = pcol_ref[g]
        prow = prow_ref[g]
        _ssd_one_direction(0, g, xf, bf, cf, dtcf, dtrf, pcol, prow, st_ref, yf_ref, row, col)
        _ssd_one_direction(1, g, xr, br, cr, dtcr, dtrr, pcol, prow, st_ref, yr_ref, row, col)


def _chunk_fwd(b, k):
    return jnp.where(k < NCH_C, NX // SSD_Q + b * NCH_C + k, b * NCH_X + (k - NCH_C))


def _chunk_rev(b, k):
    return jnp.where(k < NCH_C, NX // SSD_Q + b * NCH_C + (NCH_C - 1 - k),
                     b * NCH_X + (NCH_X - 1 - (k - NCH_C)))


def ssd_scan(xbc, p_ssd, dt_rows, pcol, prow):
    q = SSD_Q
    dtc0 = (SSD_INNER + SSD_CONV_CH) // 128

    dtcw = SSD_G * 128

    def specs(chunk):
        return [pl.BlockSpec((q, SSD_INNER), lambda b, k: (chunk(b, k), 0)),
                pl.BlockSpec((q, SSD_BC), lambda b, k: (chunk(b, k), SSD_INNER // SSD_BC)),
                pl.BlockSpec((q, SSD_BC), lambda b, k: (chunk(b, k), SSD_INNER // SSD_BC + 1)),
                pl.BlockSpec((q, dtcw), lambda b, k: (chunk(b, k), dtc0 * 128 // dtcw)),
                pl.BlockSpec((SSD_G * 8, q), lambda b, k: (0, chunk(b, k)))]

    def operands():
        return [xbc, xbc, xbc, p_ssd, dt_rows]

    out_spec_f = pl.BlockSpec((q, SSD_INNER), lambda b, k: (_chunk_fwd(b, k), 0))
    out_spec_r = pl.BlockSpec((q, SSD_INNER), lambda b, k: (_chunk_rev(b, k), 0))
    return pl.pallas_call(
        _ssd_kernel,
        grid=(NB, SSD_STEPS),
        in_specs=specs(_chunk_fwd) + specs(_chunk_rev) + [
            pl.BlockSpec((SSD_G, 8, 128), lambda b, k: (0, 0, 0)),
            pl.BlockSpec((SSD_G, 4, 8, 128), lambda b, k: (0, 0, 0, 0))],
        out_specs=[out_spec_f, out_spec_r],
        out_shape=[jax.ShapeDtypeStruct((NT, SSD_INNER), F32)] * 2,
        scratch_shapes=[pltpu.VMEM((2, SSD_G, SSD_HPG * SSD_P, SSD_N), F32)],
        compiler_params=_params(("parallel", "arbitrary")),
        name="ssd_scan",
    )(*operands(), *operands(), pcol, prow)


def _ssd_finish_kernel(yf_ref, yr_ref, xs_ref, z_ref, dsum_ref, g_ref, o_ref):
    z = z_ref[...]
    y = (yf_ref[...] + yr_ref[...] + dsum_ref[...] * xs_ref[...]) * (z * jax.nn.sigmoid(z))
    ms = jnp.mean(y * y, axis=-1, keepdims=True)
    o_ref[...] = (y * lax.rsqrt(ms + EPS) * g_ref[...]).astype(o_ref.dtype)


def ssd_finish(yf, yr, xbc, p_ssd, dsum, gain, rows):
    tm = 512
    blk = lambda: pl.BlockSpec((tm, SSD_INNER), lambda i: (i, 0))
    vec = lambda: pl.BlockSpec((1, SSD_INNER), lambda i: (0, 0))
    return pl.pallas_call(
        _ssd_finish_kernel,
        grid=(rows // tm,),
        in_specs=[blk(), blk(), blk(),
                  pl.BlockSpec((tm, SSD_INNER), lambda i: (i, SSD_CONV_CH // SSD_INNER)),
                  vec(), vec()],
        out_specs=blk(),
        out_shape=jax.ShapeDtypeStruct((rows, SSD_INNER), BF16),
        compiler_params=_params(("parallel",)),
        name="ssd_finish",
    )(yf, yr, xbc, p_ssd, dsum, gain.reshape(1, SSD_INNER))


MLA_TM = 256
LKV = LX + LC
Q_PRESCALE = (MLA_NOPE + MLA_ROPE) ** -0.5 * math.log2(math.e)


def _mla_prep_kernel(p_ref, wq_ref, wkv_ref, gq_ref, gkv_ref, tq1_ref, tq2_ref, tk1_ref, tk2_ref,
                     ct_ref, st_ref, q_ref, k_ref, v_ref):
    lane = lax.broadcasted_iota(jnp.int32, (MLA_TM, 128), 1)
    is_nope = lane < MLA_NOPE
    is_rope = jnp.logical_and(lane >= MLA_NOPE, lane < MLA_NOPE + MLA_ROPE)
    ct = ct_ref[...]
    st = st_ref[...]
    qa = p_ref[:, 0:MLA_QR]
    qn = (qa * lax.rsqrt(jnp.mean(qa * qa, axis=-1, keepdims=True) + EPS) * gq_ref[...]).astype(BF16)
    kva = p_ref[:, MLA_QR:MLA_QR + MLA_KVR]
    kvn = (kva * lax.rsqrt(jnp.mean(kva * kva, axis=-1, keepdims=True) + EPS) * gkv_ref[...]).astype(BF16)
    kp = p_ref[:, MLA_QR + MLA_KVR:]
    ss = jnp.sum(jnp.where(is_rope, kp * kp, 0.0), axis=-1, keepdims=True) * (1.0 / MLA_ROPE)
    r = lax.rsqrt(ss + EPS)
    kpe = kp * (r * ct * tk1_ref[1:2, :]) + pltpu.roll(kp, 96, 1) * (r * st * tk2_ref[...])
    rq_all = jnp.dot(qn, wq_ref[...], preferred_element_type=F32)
    rk_all = jnp.dot(kvn, wkv_ref[...], preferred_element_type=F32)
    tq1c = tq1_ref[...] * ct * Q_PRESCALE
    tq2s = tq2_ref[...] * st * Q_PRESCALE
    for h in range(MLA_H):
        rq = rq_all[:, h * 128:(h + 1) * 128]
        sq = rq * rq
        rn = lax.rsqrt(jnp.sum(jnp.where(is_nope, sq, 0.0), axis=-1, keepdims=True) * (1.0 / MLA_NOPE) + EPS)
        rp = lax.rsqrt(jnp.sum(jnp.where(is_rope, sq, 0.0), axis=-1, keepdims=True) * (1.0 / MLA_ROPE) + EPS)
        q_ref[0, h] = (rq * (jnp.where(is_nope, rn, rp) * tq1c)
                       + pltpu.roll(rq, 96, 1) * (rp * tq2s)).astype(BF16)
        rk = rk_all[:, h * 128:(h + 1) * 128]
        rkn = lax.rsqrt(jnp.sum(jnp.where(is_nope, rk * rk, 0.0), axis=-1, keepdims=True) * (1.0 / MLA_NOPE)
                        + EPS)
        k_ref[0, h] = jnp.where(is_nope, rk * rkn * tk1_ref[0:1, :], kpe).astype(BF16)
        v_ref[0, h] = rk.astype(BF16)


def _mla_tile_b(i):
    return jnp.where(i < NX // MLA_TM, i // (LX // MLA_TM), i - NX // MLA_TM)


def _mla_tile_r(i):
    return jnp.where(i < NX // MLA_TM, i % (LX // MLA_TM), LX // MLA_TM)


def mla_prep(p_mla, wq, wkv, gq, gkv, tq1, tq2, tk1, tk2, ctab, stab):
    ntile = NT // MLA_TM
    n_in = p_mla.shape[1]
    vec = lambda r: pl.BlockSpec((r, 128), lambda i: (0, 0))
    head_out = lambda: pl.BlockSpec((1, MLA_H, MLA_TM, 128), lambda i: (_mla_tile_b(i), 0, _mla_tile_r(i), 0))
    shp = jax.ShapeDtypeStruct((NB, MLA_H, LKV, 128), BF16)
    return pl.pallas_call(
        _mla_prep_kernel,
        grid=(ntile,),
        in_specs=[pl.BlockSpec((MLA_TM, n_in), lambda i: (i, 0)),
                  pl.BlockSpec((MLA_QR, MLA_H * 128), lambda i: (0, 0)),
                  pl.BlockSpec((MLA_KVR, MLA_H * 128), lambda i: (0, 0)),
                  pl.BlockSpec((1, MLA_QR), lambda i: (0, 0)),
                  pl.BlockSpec((1, MLA_KVR), lambda i: (0, 0)),
                  vec(1), vec(1), vec(2), vec(1),
                  pl.BlockSpec((MLA_TM, 128), lambda i: (i, 0)),
                  pl.BlockSpec((MLA_TM, 128), lambda i: (i, 0))],
        out_specs=[head_out(), head_out(), head_out()],
        out_shape=[shp, shp, shp],
        compiler_params=_params(("parallel",)),
        name="mla_prep",
    )(p_mla, wq, wkv, gq, gkv, tq1, tq2, tk1, tk2, ctab, stab)


ATT_TQ = 256
ATT_HEADS = 8


def _attend(q, k, v):
    s = lax.dot_general(q, k, (((1,), (1,)), ((), ())), preferred_element_type=F32)
    m = jnp.max(s, axis=-1, keepdims=True)
    e = jnp.exp2(s - m)
    l = jnp.sum(e, axis=-1, keepdims=True)
    return jnp.dot(e.astype(BF16), v, preferred_element_type=F32) / l


def _attention_kernel(q_ref, k_ref, v_ref, o_ref, *, with_ctx):
    qi = pl.program_id(2)
    lane = lax.broadcasted_iota(jnp.int32, (ATT_TQ, 128), 1)

    def run(lo):
        for p in range(ATT_HEADS // 2):
            o0 = _attend(q_ref[0, 2 * p], k_ref[0, 2 * p, lo:, :], v_ref[0, 2 * p, lo:, :])
            o1 = _attend(q_ref[0, 2 * p + 1], k_ref[0, 2 * p + 1, lo:, :], v_ref[0, 2 * p + 1, lo:, :])
            o_ref[:, p * 128:(p + 1) * 128] = jnp.where(lane < MLA_V, pltpu.roll(o0, 64, 1), o1).astype(o_ref.dtype)

    if with_ctx:
        @pl.when(qi < LX // ATT_TQ)
        def _():
            run(0)

        @pl.when(qi == LX // ATT_TQ)
        def _():
            run(LX)
    else:
        run(0)


def attention(qh, kh, vh, with_ctx):
    nq = LX // ATT_TQ + (1 if with_ctx else 0)
    rows = NT if with_ctx else NX

    def out_row(b, qi):
        return jnp.where(qi < LX // ATT_TQ, b * (LX // ATT_TQ) + qi, NX // ATT_TQ + b)

    nh = ATT_HEADS
    kv_spec = lambda: pl.BlockSpec((1, nh, LKV, 128), lambda b, p, qi: (b, p, 0, 0))
    return pl.pallas_call(
        functools.partial(_attention_kernel, with_ctx=with_ctx),
        grid=(NB, MLA_H // nh, nq),
        in_specs=[pl.BlockSpec((1, nh, ATT_TQ, 128), lambda b, p, qi: (b, p, qi, 0)), kv_spec(), kv_spec()],
        out_specs=pl.BlockSpec((ATT_TQ, nh * MLA_V), lambda b, p, qi: (out_row(b, qi), p)),
        out_shape=jax.ShapeDtypeStruct((rows, MLA_H * MLA_V), BF16),
        compiler_params=_params(("parallel", "parallel", "arbitrary")),
        name="mla_attention",
    )(qh, kh, vh)


S5_RX = NX // S5_T
S5_RC = NC // S5_T
S5_R = S5_RX + S5_RC
S5_W = S5_T * 128
S5_ST = 2 * S5_GB * S5_P


def _s5_scan_rows(s_scr, xin_scr, a_re, a_im, reverse):
    half = S5_ST // 2
    cx = LX // S5_T
    cc = LC // S5_T
    zero = jnp.zeros((1, half), F32)

    def sweep(base_of, n, state):
        def body(i, st):
            c = (n - 1 - i) if reverse else i
            new = []
            for b in range(NB):
                r = base_of(b) + c
                xr, xi = st[b]
                xin_scr[pl.ds(r, 1), :] = jnp.concatenate([xr, xi], axis=1)
                srow = s_scr[pl.ds(r, 1), :]
                new.append((a_re * xr - a_im * xi + srow[:, :half],
                            a_re * xi + a_im * xr + srow[:, half:]))
            return tuple(new)
        return lax.fori_loop(0, n, body, state)

    state = tuple((zero, zero) for _ in range(NB))
    state = sweep(lambda b: S5_RX + b * cc, cc, state)
    sweep(lambda b: b * cx, cx, state)


def _s5_kernel(u_ref, tiles_ref, win_ref, wout_ref, at_ref, y_ref, m_scr, s_scr, xin_scr):
    d = pl.program_id(1)
    ucat = jnp.concatenate([u_ref[pl.ds(t, S5_R, stride=S5_T), :] for t in range(S5_T)],
                           axis=1).astype(BF16)
    s_scr[...] = jnp.dot(ucat, win_ref[0, 0], preferred_element_type=F32)
    a_re = at_ref[0, 0][:, :S5_ST // 2]
    a_im = at_ref[0, 0][:, S5_ST // 2:]
    zero_tile = jnp.zeros((128, 128), BF16)

    def build(reverse):
        for tp in range(S5_T):
            for t in range(S5_T):
                lag = (tp - t) if reverse else (t - tp)
                m_scr[tp * 128:(tp + 1) * 128, t * 128:(t + 1) * 128] = (
                    tiles_ref[0, 0, lag] if lag >= 0 else zero_tile)

    @pl.when(d == 0)
    def _():
        build(False)
        _s5_scan_rows(s_scr, xin_scr, a_re, a_im, False)

    @pl.when(d == 1)
    def _():
        build(True)
        _s5_scan_rows(s_scr, xin_scr, a_re, a_im, True)

    ycat = (jnp.dot(ucat, m_scr[...], preferred_element_type=F32)
            + jnp.dot(xin_scr[...].astype(BF16), wout_ref[0, 0], preferred_element_type=F32))

    @pl.when(d == 0)
    def _():
        for t in range(S5_T):
            y_ref[pl.ds(t, S5_R, stride=S5_T), :] = ycat[:, t * 128:(t + 1) * 128]

    @pl.when(d == 1)
    def _():
        for t in range(S5_T):
            y_ref[pl.ds(t, S5_R, stride=S5_T), :] += ycat[:, t * 128:(t + 1) * 128]


def s5_conv(u, tiles, win, wout, at):
    nblk = S5_INNER // 128
    return pl.pallas_call(
        _s5_kernel,
        grid=(nblk, 2),
        in_specs=[pl.BlockSpec((NT, 128), lambda g, d: (0, g)),
                  pl.BlockSpec((1, 1, S5_T, 128, 128), lambda g, d: (d, g, 0, 0, 0)),
                  pl.BlockSpec((1, 1, S5_W, S5_ST), lambda g, d: (d, g, 0, 0)),
                  pl.BlockSpec((1, 1, S5_ST, S5_W), lambda g, d: (d, g, 0, 0)),
                  pl.BlockSpec((1, 1, 1, S5_ST), lambda g, d: (d, g, 0, 0))],
        out_specs=pl.BlockSpec((NT, 128), lambda g, d: (0, g)),
        out_shape=jax.ShapeDtypeStruct((NT, S5_INNER), F32),
        scratch_shapes=[pltpu.VMEM((S5_W, S5_W), BF16), pltpu.VMEM((S5_R, S5_ST), F32),
                        pltpu.VMEM((S5_R, S5_ST), F32)],
        compiler_params=_params(("parallel", "arbitrary")),
        name="s5_conv",
    )(u, tiles, win, wout, at)


def s5_operators(lam_re, lam_im, log_dt, b_re, b_im, c_re, c_im):
    t_len = S5_T
    dt = jnp.exp(log_dt)[..., None]
    mag = jnp.exp(lam_re * dt)
    ab_re = mag * jnp.cos(lam_im * dt)
    ab_im = mag * jnp.sin(lam_im * dt)
    den = lam_re * lam_re + lam_im * lam_im
    nr = ab_re - 1.0
    f_re = ((nr * lam_re + ab_im * lam_im) / den)[..., None]
    f_im = ((ab_im * lam_re - nr * lam_im) / den)[..., None]
    bb_re = f_re * b_re - f_im * b_im
    bb_im = f_re * b_im + f_im * b_re
    pr, pi = [jnp.ones_like(ab_re)], [jnp.zeros_like(ab_re)]
    for _ in range(t_len):
        pr, pi = pr + [pr[-1] * ab_re - pi[-1] * ab_im], pi + [pr[-1] * ab_im + pi[-1] * ab_re]
    pw_re = jnp.stack(pr)
    pw_im = jnp.stack(pi)
    cp_re = c_re[None] * pw_re[:, :, :, None, :] - c_im[None] * pw_im[:, :, :, None, :]
    cp_im = c_re[None] * pw_im[:, :, :, None, :] + c_im[None] * pw_re[:, :, :, None, :]
    kj = (jnp.einsum("jdgip,dgps->jdgis", cp_re[:t_len], bb_re, precision=HIGHEST)
          - jnp.einsum("jdgip,dgps->jdgis", cp_im[:t_len], bb_im, precision=HIGHEST))
    nb = S5_GROUPS // S5_GB

    def block_diag(small, src_of_col, row_group, col_group):
        c = small.shape[-1]
        place = (jnp.arange(c)[:, None] == src_of_col[None, :]).astype(F32)
        keep = row_group[:, None] == col_group[None, :]
        return jnp.where(keep, jnp.einsum("...rc,cn->...rn", small, place), 0.0)

    q128 = jnp.arange(128)
    q1k = jnp.arange(S5_ST)
    kj = kj.reshape(t_len, 2, nb, S5_GB, S5_S, S5_S).transpose(1, 2, 0, 3, 5, 4)
    tiles = block_diag(kj.reshape(2, nb, t_len, 128, S5_S), q128 % S5_S, q128 // S5_S, q128 // S5_S)
    ab_r = pw_re[:t_len, :, :, :, None] * bb_re[None] - pw_im[:t_len, :, :, :, None] * bb_im[None]
    ab_i = pw_re[:t_len, :, :, :, None] * bb_im[None] + pw_im[:t_len, :, :, :, None] * bb_re[None]
    ab = jnp.stack([ab_r, ab_i], axis=-1)
    ab = jnp.stack([ab[::-1, 0], ab[:, 1]], axis=1)
    ab = ab.reshape(t_len, 2, nb, S5_GB, S5_P, S5_S, 2).transpose(1, 2, 0, 3, 5, 6, 4)
    half = S5_ST // 2
    win = block_diag(ab.reshape(2, nb, t_len * 128, 2 * S5_P),
                     (q1k // half) * S5_P + q1k % S5_P,
                     (jnp.arange(t_len * 128) // S5_S) % S5_GB, (q1k % half) // S5_P)
    q_re = jnp.stack([cp_re[1:, 0], cp_re[:0:-1, 1]], axis=1)
    q_im = jnp.stack([cp_im[1:, 0], cp_im[:0:-1, 1]], axis=1)
    qq = jnp.stack([q_re, -q_im], axis=-1)
    qq = qq.reshape(t_len, 2, nb, S5_GB, S5_S, S5_P, 2).transpose(1, 2, 6, 3, 5, 0, 4)
    qw = jnp.arange(t_len * 128)
    wout = block_diag(qq.reshape(2, nb, S5_ST, t_len * S5_S),
                      (qw // 128) * S5_S + qw % S5_S,
                      (q1k % half) // S5_P, (qw % 128) // S5_S)
    at = jnp.stack([pw_re[t_len], pw_im[t_len]], axis=1)
    at = at.reshape(2, 2, nb, S5_GB * S5_P).transpose(0, 2, 1, 3).reshape(2, nb, 1, S5_ST)
    return tiles.astype(BF16), win.astype(BF16), wout.astype(BF16), at


def _s5_finish_kernel(y_ref, u_ref, d_ref, w_ref, o_ref):
    y = y_ref[...] + d_ref[...] * u_ref[...]
    v = jax.nn.gelu(y)
    gate = jnp.dot(v.astype(BF16), w_ref[...], preferred_element_type=F32)
    o_ref[...] = (v * jax.nn.sigmoid(gate)).astype(o_ref.dtype)


def s5_finish(y, u, d_skip, w_glu, rows):
    tm = 512
    blk = lambda: pl.BlockSpec((tm, S5_INNER), lambda i: (i, 0))
    return pl.pallas_call(
        _s5_finish_kernel,
        grid=(rows // tm,),
        in_specs=[blk(), blk(), pl.BlockSpec((1, S5_INNER), lambda i: (0, 0)),
                  pl.BlockSpec((S5_INNER, S5_INNER), lambda i: (0, 0))],
        out_specs=blk(),
        out_shape=jax.ShapeDtypeStruct((rows, S5_INNER), BF16),
        compiler_params=_params(("parallel",)),
        name="s5_finish",
    )(y, u, d_skip.reshape(1, S5_INNER), w_glu)


def _merge_kernel(ssd_ref, mla_ref, s5_ref, g1_ref, g2_ref, g3_ref, w1_ref, w2_ref, w3_ref, o_ref):
    m = (g1_ref[...] * jnp.dot(ssd_ref[...], w1_ref[...], preferred_element_type=F32)
         + g2_ref[...] * jnp.dot(mla_ref[...], w2_ref[...], preferred_element_type=F32)
         + g3_ref[...] * jnp.dot(s5_ref[...], w3_ref[...], preferred_element_type=F32))
    o_ref[...] = m.astype(o_ref.dtype)


def merge(ssd_o, mla_o, s5_o, gates, w1, w2, w3, rows):
    tm, tn = 1024, 512
    nj = D // tn
    act = lambda: pl.BlockSpec((tm, 1024), lambda i, j: (i, 0))
    wsp = lambda: pl.BlockSpec((1024, tn), lambda i, j: (0, j))
    gsp = lambda k: pl.BlockSpec((tm, tn), lambda i, j: (i, k * nj + j))
    return pl.pallas_call(
        _merge_kernel,
        grid=(rows // tm, nj),
        in_specs=[act(), act(), act(), gsp(0), gsp(1), gsp(2), wsp(), wsp(), wsp()],
        out_specs=pl.BlockSpec((tm, tn), lambda i, j: (i, j)),
        out_shape=jax.ShapeDtypeStruct((rows, D), BF16),
        compiler_params=_params(("parallel", "parallel")),
        name="merge",
    )(ssd_o, mla_o, s5_o, gates, gates, gates, w1, w2, w3)


def _outproj_kernel(*refs, tm):
    m_ref, w_ref, *x_refs, g_ref, o_ref = refs
    o_ref[...] = _pick_rows(x_refs, tm) + g_ref[0] * jnp.dot(m_ref[...], w_ref[...], preferred_element_type=F32)


def outproj(m, w_out, x, modtab, k_gate, rows, ctx=None):
    tm, tn = 1024, 1024
    tpb = LX // tm
    xs = [x] if ctx is None else [x, ctx]
    return pl.pallas_call(
        functools.partial(_outproj_kernel, tm=tm),
        grid=(rows // tm, D // tn),
        in_specs=[pl.BlockSpec((tm, D), lambda i, j: (i, 0)),
                  pl.BlockSpec((D, tn), lambda i, j: (0, j))]
        + _row_specs(tm, tn, lambda j: j, ctx is not None)
        + [pl.BlockSpec((1, 1, tn), lambda i, j: (_mod_row(i, tpb) * 6 + k_gate, 0, j))],
        out_specs=pl.BlockSpec((tm, tn), lambda i, j: (i, j)),
        out_shape=jax.ShapeDtypeStruct((rows, D), F32),
        compiler_params=_params(("parallel", "parallel")),
        name="outproj",
    )(m, w_out, *xs, modtab)


def _router_kernel(x_ref, g_ref, sc_ref, sh_ref, w_ref, h_ref, o_ref):
    x = x_ref[...]
    ms = jnp.mean(x * x, axis=-1, keepdims=True)
    h = x * lax.rsqrt(ms + EPS) * g_ref[...] * (1.0 + sc_ref[0]) + sh_ref[0]
    h_hi = h.astype(BF16)
    h_ref[...] = h_hi
    h_lo = (h - h_hi.astype(F32)).astype(BF16)
    w = w_ref[...]
    w_hi = w.astype(BF16)
    w_lo = (w - w_hi.astype(F32)).astype(BF16)
    nt = lambda a, b: lax.dot_general(a, b, (((1,), (1,)), ((), ())), preferred_element_type=F32)
    logits = nt(w_hi, h_hi) + nt(w_lo, h_hi) + nt(w_hi, h_lo)
    m = jnp.max(logits, axis=0, keepdims=True)
    e = jnp.exp(logits - m)
    o_ref[...] = e / jnp.sum(e, axis=0, keepdims=True)


def router(x, gain, modtab, w_router_t, rows):
    tm = 256
    tpb = LX // tm
    return pl.pallas_call(
        _router_kernel,
        grid=(rows // tm,),
        in_specs=[pl.BlockSpec((tm, D), lambda i: (i, 0)),
                  pl.BlockSpec((1, D), lambda i: (0, 0)),
                  pl.BlockSpec((1, 1, D), lambda i: (_mod_row(i, tpb) * 6 + 4, 0, 0)),
                  pl.BlockSpec((1, 1, D), lambda i: (_mod_row(i, tpb) * 6 + 3, 0, 0)),
                  pl.BlockSpec((N_EXP, D), lambda i: (0, 0))],
        out_specs=[pl.BlockSpec((tm, D), lambda i: (i, 0)),
                   pl.BlockSpec((N_EXP, tm), lambda i: (0, i))],
        out_shape=[jax.ShapeDtypeStruct((rows, D), BF16), jax.ShapeDtypeStruct((N_EXP, rows), F32)],
        compiler_params=_params(("parallel",)),
        name="moe_router",
    )(x, gain.reshape(1, D), modtab, modtab, w_router_t)


def _prefix_count(mask_f, n):
    row = lax.broadcasted_iota(jnp.int32, (128, 128), 0)
    col = lax.broadcasted_iota(jnp.int32, (128, 128), 1)
    upper = (row < col).astype(BF16)
    offset = jnp.zeros((N_EXP, 1), F32)
    parts = []
    for j in range(n // 128):
        blk = mask_f[:, j * 128:(j + 1) * 128]
        parts.append(jnp.dot(blk.astype(BF16), upper, preferred_element_type=F32) + offset)
        offset = offset + jnp.sum(blk, axis=1, keepdims=True)
    return jnp.concatenate(parts, axis=1)


def _select_kernel(aff_ref, g_ref, pos_ref, *, n, cap):
    aff = aff_ref[...]
    capf = float(cap)

    def count_ge(t):
        return jnp.sum((aff >= t).astype(F32), axis=1, keepdims=True)

    def bisect(_, c):
        lo, hi = c
        mid = 0.5 * (lo + hi)
        ok = count_ge(mid) >= capf
        return jnp.where(ok, mid, lo), jnp.where(ok, hi, mid)

    lo, hi = lax.fori_loop(0, 40, bisect, (jnp.zeros((N_EXP, 1), F32), jnp.full((N_EXP, 1), 2.0, F32)))

    def unfinished(c):
        return jnp.min(c[2]) < 0.5

    def step(c):
        cur, thr, done = c
        cand = jnp.max(jnp.where(aff < cur, aff, -1.0), axis=1, keepdims=True)
        ok = (count_ge(cand) >= capf).astype(F32)
        thr = jnp.where(done > 0.5, thr, cand)
        done = jnp.maximum(done, ok)
        return jnp.where(done > 0.5, cur, cand), thr, done

    _, thr, _ = lax.while_loop(unfinished, step, (hi, lo, jnp.zeros((N_EXP, 1), F32)))
    gt = aff > thr
    eq = (aff == thr).astype(F32)
    need = cap - jnp.sum(gt.astype(F32), axis=1, keepdims=True)
    sel = jnp.logical_or(gt, jnp.logical_and(eq > 0, _prefix_count(eq, n) < need))
    sel_f = sel.astype(F32)
    g_ref[...] = jnp.where(sel, aff, 0.0)
    pos_ref[...] = jnp.where(sel, _prefix_count(sel_f, n), -1.0)


def select(aff_t, n, cap, row0):
    b0 = row0 // n
    spec = lambda: pl.BlockSpec((N_EXP, n), lambda b: (0, b))
    shp = jax.ShapeDtypeStruct((N_EXP, NB * n), F32)
    return pl.pallas_call(
        functools.partial(_select_kernel, n=n, cap=cap),
        grid=(NB,),
        in_specs=[pl.BlockSpec((N_EXP, n), lambda b: (0, b0 + b))],
        out_specs=[spec(), spec()],
        out_shape=[shp, shp],
        compiler_params=_params(("parallel",)),
        name="moe_select",
    )(aff_t)


def _gather_kernel(h_ref, pos_ref, g_ref, xs_ref, gs_ref, *, n, cap):
    pos = pos_ref[0, 0]
    slot = lax.broadcasted_iota(jnp.int32, (cap, n), 0).astype(F32)
    onehot = slot == pos
    xs_ref[0] = jnp.dot(onehot.astype(BF16), h_ref[...], preferred_element_type=F32).astype(BF16)
    gsel = jnp.sum(jnp.where(onehot, g_ref[0, 0], 0.0), axis=1, keepdims=True)
    gs_ref[0] = jnp.broadcast_to(gsel, (cap, 128))


def gather(h2, pos_t, g_t, n, cap, row0):
    b0 = row0 // n
    tab = lambda: pl.BlockSpec((1, 1, 1, n), lambda b, e: (e, b, 0, 0))
    return pl.pallas_call(
        functools.partial(_gather_kernel, n=n, cap=cap),
        grid=(NB, N_EXP),
        in_specs=[pl.BlockSpec((n, D), lambda b, e: (b0 + b, 0)), tab(), tab()],
        out_specs=[pl.BlockSpec((1, cap, D), lambda b, e: (e, b, 0)),
                   pl.BlockSpec((1, cap, 128), lambda b, e: (e, b, 0))],
        out_shape=[jax.ShapeDtypeStruct((N_EXP, NB * cap, D), BF16),
                   jax.ShapeDtypeStruct((N_EXP, NB * cap, 128), F32)],
        compiler_params=_params(("parallel", "arbitrary")),
        name="moe_gather",
    )(h2, pos_t, g_t)


EXP_TF = 512
EXP_TN = 512
EXP_NF = FF // EXP_TF
EXP_NN = D // EXP_TN


def _expert_kernel(*refs, with_ctx):
    if with_ctx:
        xs_ref, xc_ref, gs_ref, gc_ref, wg_ref, wu_ref, wd_ref, ys_ref, yc_ref, hid_ref = refs
    else:
        xs_ref, gs_ref, wg_ref, wu_ref, wd_ref, ys_ref, hid_ref = refs
    f = pl.program_id(1)

    @pl.when(f < EXP_NF)
    def _():
        xs = jnp.concatenate([xs_ref[0], xc_ref[0]], axis=0) if with_ctx else xs_ref[0]
        hg = jnp.dot(xs, wg_ref[0].astype(BF16), preferred_element_type=F32)
        hu = jnp.dot(xs, wu_ref[0].astype(BF16), preferred_element_type=F32)
        hid_ref[f] = (hg * jax.nn.sigmoid(hg) * hu).astype(BF16)

    @pl.when(f >= EXP_NF)
    def _():
        acc = None
        for j in range(EXP_NF):
            part = jnp.dot(hid_ref[j], wd_ref[0, j * EXP_TF:(j + 1) * EXP_TF, :].astype(BF16),
                           preferred_element_type=F32)
            acc = part if acc is None else acc + part
        rx = NB * CAP_X
        ys_ref[0] = (acc[0:rx, :] * gs_ref[0][:, 0:1]).astype(BF16)
        if with_ctx:
            yc_ref[0] = (acc[rx:, :] * gc_ref[0][:, 0:1]).astype(BF16)


def experts(xs, gs, w_gate, w_up, w_down, layer, xc=None, gc=None):
    with_ctx = xc is not None
    rx, rc = NB * CAP_X, NB * CAP_C
    e0 = layer * N_EXP
    up_tile = lambda f: jnp.minimum(f, EXP_NF - 1)
    down_tile = lambda f: jnp.maximum(f - EXP_NF, 0)
    tok = lambda r, w: pl.BlockSpec((1, r, w), lambda e, f: (e, 0, 0))
    out_tile = lambda r: pl.BlockSpec((1, r, EXP_TN), lambda e, f: (e, 0, down_tile(f)))
    in_specs = [tok(rx, D)] + ([tok(rc, D)] if with_ctx else []) + [tok(rx, 128)] + ([tok(rc, 128)] if with_ctx else [])
    in_specs += [pl.BlockSpec((1, D, EXP_TF), lambda e, f: (e0 + e, 0, up_tile(f))),
                 pl.BlockSpec((1, D, EXP_TF), lambda e, f: (e0 + e, 0, up_tile(f))),
                 pl.BlockSpec((1, FF, EXP_TN), lambda e, f: (e0 + e, 0, down_tile(f)))]
    out_specs = [out_tile(rx)] + ([out_tile(rc)] if with_ctx else [])
    out_shape = [jax.ShapeDtypeStruct((N_EXP, rx, D), BF16)] + (
        [jax.ShapeDtypeStruct((N_EXP, rc, D), BF16)] if with_ctx else [])
    args = [xs] + ([xc] if with_ctx else []) + [gs] + ([gc] if with_ctx else []) + [w_gate, w_up, w_down]
    return pl.pallas_call(
        functools.partial(_expert_kernel, with_ctx=with_ctx),
        grid=(N_EXP, EXP_NF + EXP_NN),
        in_specs=in_specs,
        out_specs=out_specs,
        out_shape=out_shape,
        scratch_shapes=[pltpu.VMEM((EXP_NF, rx + (rc if with_ctx else 0), EXP_TF), BF16)],
        compiler_params=_params(("parallel", "arbitrary")),
        name="moe_experts",
    )(*args)


def _combine_kernel(*refs, cap, aliased):
    pos_ref, ys_ref, x_ref, g_ref, o_ref = refs[1:] if aliased else refs
    pos = pos_ref[...]
    tm = pos.shape[0]
    cap_pad = max(cap, 128)
    slot = lax.broadcasted_iota(jnp.int32, (tm, cap_pad), 1).astype(F32)
    acc = None
    for e in range(N_EXP):
        onehot = (pos[:, e:e + 1] == slot).astype(BF16)
        ys = ys_ref[e]
        if cap_pad > cap:
            ys = jnp.concatenate([ys, jnp.zeros((cap_pad - cap, D), BF16)], axis=0)
        part = jnp.dot(onehot, ys, preferred_element_type=F32)
        acc = part if acc is None else acc + part
    o_ref[...] = x_ref[...] + g_ref[0] * acc


def combine(pos_col, ys, x, modtab, k_gate, n, cap, row0, mod_row_of_batch, out_rows, prev=None):
    tm = 256
    tpb = n // tm
    r0 = row0 // tm
    aliased = prev is not None
    in_specs = [pl.BlockSpec((tm, N_EXP), lambda b, t: (b * tpb + t, 0)),
                pl.BlockSpec((N_EXP, cap, D), lambda b, t: (0, b, 0)),
                pl.BlockSpec((tm, D), lambda b, t: (r0 + b * tpb + t, 0)),
                pl.BlockSpec((1, 1, D), lambda b, t: (mod_row_of_batch(b) * 6 + k_gate, 0, 0))]
    args = [pos_col, ys, x, modtab]
    if aliased:
        in_specs = [pl.BlockSpec(memory_space=pl.ANY)] + in_specs
        args = [prev] + args
    return pl.pallas_call(
        functools.partial(_combine_kernel, cap=cap, aliased=aliased),
        grid=(NB, tpb),
        in_specs=in_specs,
        out_specs=pl.BlockSpec((tm, D), lambda b, t: (r0 + b * tpb + t, 0)),
        out_shape=jax.ShapeDtypeStruct((out_rows, D), F32),
        input_output_aliases={0: 0} if aliased else {},
        compiler_params=_params(("parallel", "parallel")),
        name="moe_combine",
    )(*args)


def moe_layer(xa, gain, modtab, w_router_t, w_gate, w_up, w_down, layer, with_ctx):
    rows = NT if with_ctx else NX
    h2, aff = router(xa, gain, modtab, w_router_t, rows)

    def route(n, cap, row0):
        g_t, pos_t = select(aff, n, cap, row0)
        xs, gs = gather(h2, pos_t.reshape(N_EXP, NB, 1, n), g_t.reshape(N_EXP, NB, 1, n), n, cap, row0)
        return pos_t.T, xs, gs

    pos_x, xs_x, gs_x = route(LX, CAP_X, 0)
    if with_ctx:
        pos_c, xs_c, gs_c = route(LC, CAP_C, NX)
        ys_x, ys_c = experts(xs_x, gs_x, w_gate, w_up, w_down, layer, xs_c, gs_c)
    else:
        (ys_x,) = experts(xs_x, gs_x, w_gate, w_up, w_down, layer)
    out = combine(pos_x, ys_x, xa, modtab, 5, LX, CAP_X, 0, lambda b: b, rows)
    if with_ctx:
        out = combine(pos_c, ys_c, xa, modtab, 5, LC, CAP_C, NX, lambda b: NB, rows, prev=out)
    return out


def _rope_tables():
    n_freq = MLA_ROPE // 4
    inv = ROPE_BASE ** (-jnp.arange(n_freq, dtype=F32) / n_freq)
    rows = jnp.repeat(jnp.arange(LX // GRID_W, dtype=F32), GRID_W)
    cols = jnp.tile(jnp.arange(GRID_W, dtype=F32), LX // GRID_W)
    ang = jnp.concatenate([rows[:, None] * inv, cols[:, None] * inv], axis=-1)
    cos, sin = jnp.cos(ang), jnp.sin(ang)
    one = jnp.ones((LX, MLA_NOPE), F32)
    zero = jnp.zeros((LX, 32), F32)
    ct_x = jnp.concatenate([one, cos, cos, zero], axis=1)
    st_x = jnp.concatenate([0 * one, -sin, sin, zero], axis=1)
    ct_c = jnp.concatenate([jnp.ones((NC, 96), F32), jnp.zeros((NC, 32), F32)], axis=1)
    ctab = jnp.concatenate([jnp.tile(ct_x, (NB, 1)), ct_c], axis=0)
    stab = jnp.concatenate([jnp.tile(st_x, (NB, 1)), jnp.zeros((NC, 128), F32)], axis=0)
    return ctab, stab


_PERM = list(range(0, MLA_ROPE, 2)) + list(range(1, MLA_ROPE, 2))
_PERM_SW = list(range(1, MLA_ROPE, 2)) + list(range(0, MLA_ROPE, 2))


def _lane_table(first64, rope32):
    return jnp.concatenate([first64, rope32, jnp.zeros((32,), F32)]).reshape(1, 128)


def _layer_weights(i, w_in, ssd_a_log, ssd_dt_bias, mla_w_q_b, mla_w_kv_b, mla_q_gain, mla_k_gain):
    w = w_in[i]
    wdt = w[:, SSD_INNER + SSD_CONV_CH:O1].reshape(D, SSD_G, SSD_HPG)
    wdt_col = jnp.pad(wdt, ((0, 0), (0, 0), (0, 128 - SSD_HPG))).reshape(D, SSD_G * 128)
    w_ssd = jnp.concatenate([w[:, SSD_INNER:SSD_INNER + SSD_CONV_CH], w[:, :SSD_INNER], wdt_col],
                            axis=1).astype(BF16)
    wdt_row = jnp.pad(jnp.transpose(wdt, (1, 2, 0)), ((0, 0), (0, 8 - SSD_HPG), (0, 0))).reshape(SSD_G * 8, D)
    kpe = w[:, O1 + MLA_QR + MLA_KVR:O2]
    kpe128 = jnp.concatenate([jnp.zeros((D, MLA_NOPE), F32), kpe[:, _PERM], kpe[:, _PERM_SW]], axis=1)
    w_mla = jnp.concatenate([w[:, O1:O1 + MLA_QR + MLA_KVR], kpe128], axis=1).astype(BF16)
    w_s5 = w[:, O2:O3].astype(BF16)
    w_gate = w[:, O3:].astype(BF16)
    par = jnp.stack([ssd_a_log[i, 0], ssd_a_log[i, 1], ssd_dt_bias[i, 0], ssd_dt_bias[i, 1]])
    par = par.reshape(4, SSD_G, SSD_HPG).transpose(1, 0, 2)
    pcol = jnp.pad(par, ((0, 0), (0, 4), (0, 128 - SSD_HPG)))
    prow = jnp.broadcast_to(jnp.pad(par, ((0, 0), (0, 0), (0, 8 - SSD_HPG)))[..., None], (SSD_G, 4, 8, 128))
    wq = mla_w_q_b[i].reshape(MLA_QR, MLA_H, MLA_NOPE + MLA_ROPE)
    wq = jnp.concatenate([wq[..., :MLA_NOPE], wq[..., MLA_NOPE:][..., _PERM], wq[..., MLA_NOPE:][..., _PERM_SW]],
                         axis=-1).reshape(MLA_QR, MLA_H * 128).astype(BF16)
    wkv = mla_w_kv_b[i].astype(BF16)
    qg, kg = mla_q_gain[i], mla_k_gain[i]
    tq1 = _lane_table(qg[:MLA_NOPE], qg[MLA_NOPE:][jnp.array(_PERM)])
    tq2 = _lane_table(jnp.zeros((MLA_NOPE,), F32), qg[MLA_NOPE:][jnp.array(_PERM_SW)])
    tk1 = jnp.concatenate([_lane_table(kg[:MLA_NOPE], jnp.zeros((MLA_ROPE,), F32)),
                           _lane_table(jnp.zeros((MLA_NOPE,), F32), kg[MLA_NOPE:][jnp.array(_PERM)])], axis=0)
    tk2 = _lane_table(jnp.zeros((MLA_NOPE,), F32), kg[MLA_NOPE:][jnp.array(_PERM_SW)])
    return dict(w_ssd=w_ssd, wdt_row=wdt_row.astype(BF16), w_mla=w_mla, w_s5=w_s5, w_gate=w_gate,
                pcol=pcol, prow=prow, wq=wq, wkv=wkv, tq1=tq1, tq2=tq2, tk1=tk1, tk2=tk2)


def kernel(x, c, ctx, c_ctx, norm1_gain, norm2_gain, w_mod, b_mod, w_in, ssd_conv_w, ssd_conv_b, ssd_a_log, ssd_dt_bias, ssd_d, ssd_norm_gain, mla_q_a_gain, mla_kv_a_gain, mla_w_q_b, mla_w_kv_b, mla_q_gain, mla_k_gain, s5_lam_re, s5_lam_im, s5_log_dt, s5_b_re, s5_b_im, s5_c_re, s5_c_im, s5_d, s5_w_glu, w_branch_ssd, w_branch_mla, w_branch_s5, w_out, moe_router, moe_w_gate, moe_w_up, moe_w_down):
    ctab, stab = _rope_tables()
    cvec = jnp.concatenate([c, c_ctx[None], jnp.zeros((8 - NB - 1, D), F32)], axis=0)
    mod = modulation(cvec, w_mod, b_mod)
    xa, xc = x.reshape(NX, D), ctx.reshape(NC, D)
    for i in range(DEPTH):
        need_ctx = i < DEPTH - 1
        rows = NT if need_ctx else NX
        modtab = mod[i].reshape(8 * 6, 1, D)
        lw = _layer_weights(i, w_in, ssd_a_log, ssd_dt_bias, mla_w_q_b, mla_w_kv_b, mla_q_gain, mla_k_gain)
        h = normmod(xa, norm1_gain[i], modtab, 0, 1, NT, ctx=xc)
        p_ssd = matmul(h, lw["w_ssd"], 1024, 896, name="inproj_ssd")
        dt_rows = matmul_nt(lw["wdt_row"], h, 1024)
        p_mla = matmul(h, lw["w_mla"], 1024, 896, name="inproj_mla")
        u_s5 = matmul(h, lw["w_s5"], 1024, 1024, name="inproj_s5")
        gates = matmul(h, lw["w_gate"], 1024, 1024, act="sigmoid", out_dtype=BF16, name="inproj_gate", rows=rows)
        xbc = ssd_conv(p_ssd, ssd_conv_w[i], ssd_conv_b[i])
        yf, yr = ssd_scan(xbc, p_ssd, dt_rows, lw["pcol"], lw["prow"])
        dsum = jnp.repeat(ssd_d[i, 0] + ssd_d[i, 1], SSD_P).reshape(1, SSD_INNER)
        ssd_o = ssd_finish(yf, yr, xbc, p_ssd, dsum, ssd_norm_gain[i], rows)
        qh, kh, vh = mla_prep(p_mla, lw["wq"], lw["wkv"], mla_q_a_gain[i].reshape(1, MLA_QR),
                              mla_kv_a_gain[i].reshape(1, MLA_KVR), lw["tq1"], lw["tq2"], lw["tk1"], lw["tk2"],
                              ctab, stab)
        mla_o = attention(qh, kh, vh, need_ctx)
        ops = s5_operators(s5_lam_re[i], s5_lam_im[i], s5_log_dt[i], s5_b_re[i], s5_b_im[i], s5_c_re[i], s5_c_im[i])
        y_s5 = s5_conv(u_s5, *ops)
        s5_o = s5_finish(y_s5, u_s5, s5_d[i], s5_w_glu[i].astype(BF16), rows)
        m = merge(ssd_o, mla_o, s5_o, gates, w_branch_ssd[i].astype(BF16), w_branch_mla[i].astype(BF16),
                  w_branch_s5[i].astype(BF16), rows)
        xa = outproj(m, w_out[i].astype(BF16), xa, modtab, 2, rows, ctx=xc)
        xc = None
        xa = moe_layer(xa, norm2_gain[i], modtab, moe_router[i].T, moe_w_gate.reshape(DEPTH * N_EXP, D, FF),
                       moe_w_up.reshape(DEPTH * N_EXP, D, FF), moe_w_down.reshape(DEPTH * N_EXP, FF, D), i, need_ctx)
    return xa[:NX].reshape(NB, LX, D)
```
